```python
import math
import jax, jax.numpy as jnp
from jax import lax
import numpy as np

D_MODEL = 1024
BATCH = 8
SEQ = 2048
DEPTH = 4

D_MIX = D_MODEL
EPS = 1e-6
GLA_HEADS = 4
GLA_DV_TOTAL = D_MIX // 2
GLA_DK_TOTAL = GLA_DV_TOTAL // 2
GLA_DK = GLA_DK_TOTAL // GLA_HEADS
GLA_DV = GLA_DV_TOTAL // GLA_HEADS
GLA_LOW_RANK = 16
GLA_GATE_NORM = 16.0
GLA_CHUNK = 64
S5_WIDTH = D_MIX // 4
S5_GROUP = 16
S5_GROUPS = S5_WIDTH // S5_GROUP
S5_STATE = 64
S5_DT_MIN = 1e-3
S5_DT_MAX = 1e-1
RG_WIDTH = D_MIX // 4
RG_HEADS = 4
RG_HEAD_DIM = RG_WIDTH // RG_HEADS
RG_CONV = 4
RG_C = 8.0
D_FF = 2816
FFN_CONV = 3
IN_SIZES = (GLA_DK_TOTAL, GLA_DK_TOTAL, GLA_DV_TOTAL, GLA_DV_TOTAL, GLA_LOW_RANK, S5_WIDTH, RG_WIDTH, RG_WIDTH)
D_IN = 2 * GLA_DK_TOTAL + 2 * GLA_DV_TOTAL + GLA_LOW_RANK + S5_WIDTH + 2 * RG_WIDTH

kernel_name = "hymba_style_gla_s5_rglru_hybrid"


def rmsnorm(x, g):
    xf = x.astype(jnp.float32)
    y = xf * lax.rsqrt(jnp.mean(xf * xf, axis=-1, keepdims=True) + EPS)
    return y.astype(x.dtype) * g


def causal_dwconv(x, w, b):
    k, c = w.shape
    y = lax.conv_general_dilated(x, w[:, None, :].astype(x.dtype), window_strides=(1,),
                                 padding=[(k - 1, 0)], dimension_numbers=('NWC', 'WIO', 'NWC'),
                                 feature_group_count=c)
    return y + b


def linear_rec_combine(e1, e2):
    a1, x1 = e1
    a2, x2 = e2
    return a2 * a1, a2 * x1 + x2


def gla_mixer(q, k, v, g, gk_low, w_gk_up, b_gk, norm_g):
    B, L, _ = q.shape
    H, DK, DV, C = GLA_HEADS, GLA_DK, GLA_DV, GLA_CHUNK
    n = L // C
    f32 = jnp.float32
    gk = jax.nn.log_sigmoid((gk_low @ w_gk_up + b_gk).astype(f32)) / GLA_GATE_NORM

    def chunks(t, d):
        return t.reshape(B, n, C, H, d).transpose(0, 3, 1, 2, 4)

    qc = chunks(q.astype(f32) * (DK ** -0.5), DK)
    kc = chunks(k.astype(f32), DK)
    vc = chunks(v.astype(f32), DV)
    bc = jnp.cumsum(chunks(gk, DK), axis=3)
    b_mid = bc[:, :, :, C // 2 - 1:C // 2]
    scores = jnp.einsum('bhncd,bhnsd->bhncs', qc * jnp.exp(bc - b_mid), kc * jnp.exp(b_mid - bc))
    causal = jnp.tril(jnp.ones((C, C), dtype=bool))
    scores = jnp.where(causal, scores, 0.0)
    o_intra = jnp.einsum('bhncs,bhnsv->bhncv', scores, vc)
    b_last = bc[:, :, :, -1]
    kv = jnp.einsum('bhnsd,bhnsv->bhndv', kc * jnp.exp(b_last[:, :, :, None] - bc), vc)
    decay = jnp.exp(b_last)

    def step(S, inp):
        dec, kv_n = inp
        return dec[..., None] * S + kv_n, S

    S0 = jnp.zeros((B, H, DK, DV), f32)
    _, S_prev = lax.scan(step, S0, (jnp.moveaxis(decay, 2, 0), jnp.moveaxis(kv, 2, 0)))
    S_prev = jnp.moveaxis(S_prev, 0, 2)
    o_inter = jnp.einsum('bhncd,bhndv->bhncv', qc * jnp.exp(bc), S_prev)
    o = (o_intra + o_inter).transpose(0, 2, 3, 1, 4).reshape(B, L, H, DV)
    o = rmsnorm(o, norm_g).reshape(B, L, H * DV).astype(q.dtype)
    return o * jax.nn.silu(g)


def s5_mixer(u, lam_re, lam_im, log_dt, b_re, b_im, c_re, c_im, d, w_glu, b_glu):
    B, L, _ = u.shape
    f32 = jnp.float32
    uf = u.astype(f32)
    ug = uf.reshape(B, L, S5_GROUPS, S5_GROUP).astype(jnp.complex64)
    lam = lax.complex(lam_re.astype(f32), lam_im.astype(f32))
    dt = jnp.exp(log_dt.astype(f32))[:, None]
    lam_bar = jnp.exp(lam * dt)
    b = lax.complex(b_re.astype(f32), b_im.astype(f32))
    b_bar = ((lam_bar - 1.0) / lam)[..., None] * b
    bu = jnp.einsum('gph,blgh->blgp', b_bar, ug)
    a = jnp.broadcast_to(lam_bar, bu.shape)
    _, states = lax.associative_scan(linear_rec_combine, (a, bu), axis=1)
    c = lax.complex(c_re.astype(f32), c_im.astype(f32))
    y = jnp.einsum('ghp,blgp->blgh', c, states).real.reshape(B, L, S5_WIDTH) + d * uf
    y = jax.nn.gelu(y)
    y = y * jax.nn.sigmoid(y @ w_glu + b_glu)
    return y.astype(u.dtype)


def rglru_mixer(xr, gate, conv_w, conv_b, w_a, b_a, w_x, b_x, lam):
    B, L, _ = xr.shape
    f32 = jnp.float32
    xc = causal_dwconv(xr, conv_w, conv_b)
    xh = xc.reshape(B, L, RG_HEADS, RG_HEAD_DIM)
    r = jax.nn.sigmoid(jnp.einsum('blhi,hij->blhj', xh, w_a).reshape(B, L, RG_WIDTH) + b_a)
    i = jax.nn.sigmoid(jnp.einsum('blhi,hij->blhj', xh, w_x).reshape(B, L, RG_WIDTH) + b_x)
    log_a = -RG_C * r.astype(f32) * jax.nn.softplus(-lam.astype(f32))
    a = jnp.exp(log_a)
    xin = jnp.sqrt(-jnp.expm1(2.0 * log_a)) * (i * xc).astype(f32)
    _, h = lax.associative_scan(linear_rec_combine, (a, xin), axis=1)
    return h.astype(xr.dtype) * jax.nn.gelu(gate)


def conv_ffn(h, w_up, conv_w, conv_b, w_down):
    up, gv = jnp.split(h @ w_up, 2, axis=-1)
    up = causal_dwconv(up, conv_w, conv_b)
    return (jax.nn.gelu(up) * gv) @ w_down


def setup_inputs(seed: int = 0) -> dict:
    key = jax.random.key(seed)
    ks = jax.random.split(key, 32)
    f32 = jnp.float32
    L = DEPTH
    G, P = S5_GROUPS, S5_STATE

    def nrm(k, shape, scale):
        return scale * jax.random.normal(k, shape, f32)

    def gain(k, shape):
        return 1.0 + 0.02 * jax.random.normal(k, shape, f32)

    a0 = jax.random.uniform(ks[23], (L, RG_WIDTH), f32, 0.9, 0.999)
    return {
        "x": jax.random.normal(ks[0], (BATCH, SEQ, D_MODEL), f32),
        "attn_norm": gain(ks[1], (L, D_MODEL)),
        "w_in": nrm(ks[2], (L, D_MODEL, D_IN), D_MODEL ** -0.5),
        "gla_w_gk_up": nrm(ks[3], (L, GLA_LOW_RANK, GLA_DK_TOTAL), GLA_LOW_RANK ** -0.5),
        "gla_b_gk": nrm(ks[4], (L, GLA_DK_TOTAL), 0.1),
        "gla_norm": gain(ks[5], (L, GLA_DV)),
        "s5_lambda_re": -0.5 + nrm(ks[6], (L, G, P), 0.01),
        "s5_lambda_im": jnp.pi * jnp.arange(P, dtype=f32) + nrm(ks[7], (L, G, P), 0.01),
        "s5_log_dt": jax.random.uniform(ks[8], (L, G), f32, math.log(S5_DT_MIN), math.log(S5_DT_MAX)),
        "s5_b_re": nrm(ks[9], (L, G, P, S5_GROUP), (2 * S5_GROUP) ** -0.5),
        "s5_b_im": nrm(ks[10], (L, G, P, S5_GROUP), (2 * S5_GROUP) ** -0.5),
        "s5_c_re": nrm(ks[11], (L, G, S5_GROUP, P), (2 * P) ** -0.5),
        "s5_c_im": nrm(ks[12], (L, G, S5_GROUP, P), (2 * P) ** -0.5),
        "s5_d": nrm(ks[13], (L, S5_WIDTH), 1.0),
        "s5_w_glu": nrm(ks[14], (L, S5_WIDTH, S5_WIDTH), S5_WIDTH ** -0.5),
        "s5_b_glu": nrm(ks[15], (L, S5_WIDTH), 0.01),
        "s5_norm": gain(ks[16], (L, S5_WIDTH)),
        "rg_conv_w": nrm(ks[17], (L, RG_CONV, RG_WIDTH), RG_CONV ** -0.5),
        "rg_conv_b": nrm(ks[18], (L, RG_WIDTH), 0.01),
        "rg_w_a": nrm(ks[19], (L, RG_HEADS, RG_HEAD_DIM, RG_HEAD_DIM), RG_HEAD_DIM ** -0.5),
        "rg_b_a": nrm(ks[20], (L, RG_WIDTH), 0.01),
        "rg_w_x": nrm(ks[21], (L, RG_HEADS, RG_HEAD_DIM, RG_HEAD_DIM), RG_HEAD_DIM ** -0.5),
        "rg_b_x": nrm(ks[22], (L, RG_WIDTH), 0.01),
        "rg_lambda": jnp.log(a0) - jnp.log1p(-a0),
        "rg_norm": gain(ks[24], (L, RG_WIDTH)),
        "w_out": nrm(ks[25], (L, D_MIX, D_MODEL), D_MIX ** -0.5),
        "mlp_norm": gain(ks[26], (L, D_MODEL)),
        "w_up": nrm(ks[27], (L, D_MODEL, 2 * D_FF), D_MODEL ** -0.5),
        "mlp_conv_w": nrm(ks[28], (L, FFN_CONV, D_FF), FFN_CONV ** -0.5),
        "mlp_conv_b": nrm(ks[29], (L, D_FF), 0.01),
        "w_down": nrm(ks[30], (L, D_FF, D_MODEL), D_FF ** -0.5),
        "final_norm": gain(ks[31], (D_MODEL,)),
    }


def reference(x, attn_norm, w_in, gla_w_gk_up, gla_b_gk, gla_norm,
              s5_lambda_re, s5_lambda_im, s5_log_dt, s5_b_re, s5_b_im, s5_c_re, s5_c_im,
              s5_d, s5_w_glu, s5_b_glu, s5_norm,
              rg_conv_w, rg_conv_b, rg_w_a, rg_b_a, rg_w_x, rg_b_x, rg_lambda, rg_norm,
              w_out, mlp_norm, w_up, mlp_conv_w, mlp_conv_b, w_down, final_norm):
    splits = np.cumsum(np.array(IN_SIZES))[:-1].tolist()
    for l in range(DEPTH):
        h = rmsnorm(x, attn_norm[l])
        q, k, v, g, gk_low, s5_u, rg_x, rg_gate = jnp.split(h @ w_in[l], splits, axis=-1)
        y_gla = gla_mixer(q, k, v, g, gk_low, gla_w_gk_up[l], gla_b_gk[l], gla_norm[l])
        y_s5 = rmsnorm(s5_mixer(s5_u, s5_lambda_re[l], s5_lambda_im[l], s5_log_dt[l], s5_b_re[l], s5_b_im[l],
                                s5_c_re[l], s5_c_im[l], s5_d[l], s5_w_glu[l], s5_b_glu[l]), s5_norm[l])
        y_rg = rmsnorm(rglru_mixer(rg_x, rg_gate, rg_conv_w[l], rg_conv_b[l], rg_w_a[l], rg_b_a[l],
                                   rg_w_x[l], rg_b_x[l], rg_lambda[l]), rg_norm[l])
        x = x + jnp.concatenate([y_gla, y_s5, y_rg], axis=-1) @ w_out[l]
        h = rmsnorm(x, mlp_norm[l])
        x = x + conv_ffn(h, w_up[l], mlp_conv_w[l], mlp_conv_b[l], w_down[l])
    return rmsnorm(x, final_norm)
```

```python
import functools
import math

import jax
import jax.numpy as jnp
from jax import lax
from jax.experimental import pallas as pl
from jax.experimental.pallas import tpu as pltpu

D_MODEL = 1024
BATCH = 8
SEQ = 2048
DEPTH = 4
EPS = 1e-6
GLA_HEADS = 4
GLA_DK_TOTAL = 256
GLA_DV_TOTAL = 512
GLA_DK = 64
GLA_DV = 128
GLA_LOW_RANK = 16
GLA_GATE_NORM = 16.0
GLA_CHUNK = 64
S5_WIDTH = 256
S5_GROUP = 16
S5_GROUPS = 16
S5_STATE = 64
S5_FLAT = S5_GROUPS * S5_STATE
RG_WIDTH = 256
RG_HEADS = 4
RG_HEAD_DIM = 64
RG_CONV = 4
RG_C = 8.0
D_FF = 2816
FFN_CONV = 3

LANES = 128
SUBLANES = 8
VMEM_LIMIT = 56 * 1024 * 1024

OFF_Q = 0
OFF_K = 256
OFF_V = 512
OFF_G = 1024
OFF_S5 = 1536
OFF_RGX = 1792
OFF_RGG = 2048
OFF_GKL = 2304
D_PROJ = 2432

TC = GLA_CHUNK
ROWS = BATCH * TC
TM = 512
FF_CHUNK = 256

V_BGK, V_S5D, V_BGLU, V_S5N, V_CW0, V_CB, V_BA, V_BX, V_LAM, V_RGN = 0, 1, 2, 3, 4, 8, 9, 10, 11, 12

f32 = jnp.float32
bf16 = jnp.bfloat16


def _rms(x, g):
    return x * lax.rsqrt(jnp.mean(x * x, axis=-1, keepdims=True) + EPS) * g


def _softplus(x):
    return jnp.maximum(x, 0.0) + jnp.log1p(jnp.exp(-jnp.abs(x)))


def _dot(a, b):
    return jnp.dot(a, b, preferred_element_type=f32)


def _dot_nt(a, b):
    return lax.dot_general(a, b, (((1,), (1,)), ((), ())), preferred_element_type=f32)


def _dot_tn(a, b):
    return lax.dot_general(a, b, (((0,), (0,)), ((), ())), preferred_element_type=f32)


def _inproj_kernel(x_ref, g_ref, w_ref, o_ref):
    h = _rms(x_ref[...], g_ref[...]).astype(bf16)
    o_ref[...] = _dot(h, w_ref[...])


def _inproj(x2, attn_norm, w_in, layer):
    rows = x2.shape[0]
    return pl.pallas_call(
        _inproj_kernel,
        grid=(rows // TM,),
        in_specs=[
            pl.BlockSpec((TM, D_MODEL), lambda i: (i, 0)),
            pl.BlockSpec((None, 1, D_MODEL), lambda i: (layer, 0, 0)),
            pl.BlockSpec((None, D_MODEL, D_PROJ), lambda i: (layer, 0, 0)),
        ],
        out_specs=pl.BlockSpec((TM, D_PROJ), lambda i: (i, 0)),
        out_shape=jax.ShapeDtypeStruct((rows, D_PROJ), f32),
        compiler_params=pltpu.CompilerParams(
            dimension_semantics=("arbitrary",), vmem_limit_bytes=VMEM_LIMIT),
        name="inproj",
    )(x2, attn_norm, w_in)


def _put_tiles(ref, val):
    for j in range(ref.shape[0]):
        ref[j] = val[:, j * LANES:(j + 1) * LANES]


def _get_tiles(ref):
    return jnp.concatenate([ref[j] for j in range(ref.shape[0])], axis=-1)


def _mixer_kernel(p_ref, x_ref, wgk_ref, vec_ref, gn_ref, lam_ref, bd_ref, cd_ref,
                  wglu_ref, wa_ref, wx_ref, wout_ref, o_ref,
                  st_ref, s5_ref, rgh_ref, rgprev_ref, bu_ref, a_ref, xin_ref, y_ref):
    step = pl.program_id(0)

    @pl.when(step == 0)
    def _():
        st_ref[...] = jnp.zeros_like(st_ref)
        s5_ref[...] = jnp.zeros_like(s5_ref)
        rgh_ref[...] = jnp.zeros_like(rgh_ref)
        rgprev_ref[...] = jnp.zeros_like(rgprev_ref)

    def vec(r):
        return vec_ref[r:r + 1, :]

    def seg(off, width):
        return p_ref[:, :, off:off + width].reshape(ROWS, width)

    u = seg(OFF_S5, S5_WIDTH)
    _put_tiles(bu_ref, _dot(u.astype(bf16), bd_ref[...]))

    rx3 = p_ref[:, :, OFF_RGX:OFF_RGX + RG_WIDTH]
    prev3 = rgprev_ref[...]
    t_idx = lax.broadcasted_iota(jnp.int32, (BATCH, TC, RG_WIDTH), 1)
    xc3 = rx3 * vec(V_CW0 + RG_CONV - 1) + vec(V_CB)
    for s in range(1, RG_CONV):
        shifted = jnp.where(t_idx < s, pltpu.roll(prev3, s, 1), pltpu.roll(rx3, s, 1))
        xc3 = xc3 + shifted * vec(V_CW0 + RG_CONV - 1 - s)
    rgprev_ref[...] = rx3
    xc = xc3.reshape(ROWS, RG_WIDTH)
    xcb = xc.astype(bf16)
    r = jax.nn.sigmoid(_dot(xcb, wa_ref[...]) + vec(V_BA))
    i = jax.nn.sigmoid(_dot(xcb, wx_ref[...]) + vec(V_BX))
    log_a = -RG_C * r * _softplus(-vec(V_LAM))
    a = jnp.exp(log_a)
    _put_tiles(a_ref, a)
    _put_tiles(xin_ref, jnp.sqrt(-jnp.tanh(log_a) * (a * a + 1.0)) * (i * xc))

    n_s5 = S5_FLAT // LANES
    n_rg = RG_WIDTH // LANES
    lre = [jnp.broadcast_to(lam_ref[0:1, j * LANES:(j + 1) * LANES], (BATCH, LANES)) for j in range(n_s5)]
    lim = [jnp.broadcast_to(lam_ref[1:2, j * LANES:(j + 1) * LANES], (BATCH, LANES)) for j in range(n_s5)]

    def scan_step(t, carry):
        sre, sim, h = carry
        rows = pl.ds(t, BATCH, stride=TC)
        nre, nim, nh = [], [], []
        for j in range(n_s5):
            re = lre[j] * sre[j] - lim[j] * sim[j] + bu_ref[j, rows, :]
            im = lre[j] * sim[j] + lim[j] * sre[j] + bu_ref[n_s5 + j, rows, :]
            bu_ref[j, rows, :] = re
            bu_ref[n_s5 + j, rows, :] = im
            nre.append(re)
            nim.append(im)
        for j in range(n_rg):
            hj = a_ref[j, rows, :] * h[j] + xin_ref[j, rows, :]
            xin_ref[j, rows, :] = hj
            nh.append(hj)
        return tuple(nre), tuple(nim), tuple(nh)

    init = (tuple(s5_ref[j] for j in range(n_s5)),
            tuple(s5_ref[n_s5 + j] for j in range(n_s5)),
            tuple(rgh_ref[j] for j in range(n_rg)))
    sre, sim, h = lax.fori_loop(0, TC, scan_step, init)
    for j in range(n_s5):
        s5_ref[j] = sre[j]
        s5_ref[n_s5 + j] = sim[j]
    for j in range(n_rg):
        rgh_ref[j] = h[j]

    ys = _dot(_get_tiles(bu_ref).astype(bf16), cd_ref[...]) + vec(V_S5D) * u
    ys = jax.nn.gelu(ys)
    ys = ys * jax.nn.sigmoid(_dot(ys.astype(bf16), wglu_ref[...]) + vec(V_BGLU))
    y_ref[:, GLA_DV_TOTAL:GLA_DV_TOTAL + S5_WIDTH] = _rms(ys, vec(V_S5N)).astype(bf16)

    yr = _get_tiles(xin_ref) * jax.nn.gelu(seg(OFF_RGG, RG_WIDTH))
    y_ref[:, GLA_DV_TOTAL + S5_WIDTH:] = _rms(yr, vec(V_RGN)).astype(bf16)

    gk_pre = jnp.dot(seg(OFF_GKL, LANES), wgk_ref[...], preferred_element_type=f32,
                     precision=lax.Precision.HIGHEST) + vec(V_BGK)
    gk = (jnp.minimum(gk_pre, 0.0) - jnp.log1p(jnp.exp(-jnp.abs(gk_pre)))) / GLA_GATE_NORM
    ri = lax.broadcasted_iota(jnp.int32, (ROWS, ROWS), 0)
    ci = lax.broadcasted_iota(jnp.int32, (ROWS, ROWS), 1)
    tril = jnp.where((ri - ci >= 0) & (ri - ci <= (ri & (TC - 1))), 1.0, 0.0).astype(f32)
    bc = jnp.dot(tril, gk, preferred_element_type=f32, precision=lax.Precision.HIGHEST)
    bc3 = bc.reshape(BATCH, TC, GLA_DK_TOTAL)
    bmid = bc3[:, TC // 2 - 1:TC // 2, :]
    blast = bc3[:, TC - 1:TC, :]
    q3 = p_ref[:, :, OFF_Q:OFF_Q + GLA_DK_TOTAL] * (GLA_DK ** -0.5)
    k3 = p_ref[:, :, OFF_K:OFF_K + GLA_DK_TOTAL]
    qs3 = q3 * jnp.exp(bc3 - bmid)
    kt3 = (k3 * jnp.exp(bmid - bc3)).astype(bf16)
    qe3 = (q3 * jnp.exp(bc3)).astype(bf16)
    kl3 = (k3 * jnp.exp(blast - bc3)).astype(bf16)
    decay3 = jnp.exp(blast)

    lane_head = lax.broadcasted_iota(jnp.int32, (TC, GLA_DK_TOTAL), 1) >> 6
    causal = (lax.broadcasted_iota(jnp.int32, (TC, TC), 1)
              <= lax.broadcasted_iota(jnp.int32, (TC, TC), 0))
    st_mask = (lax.broadcasted_iota(jnp.int32, (GLA_DV_TOTAL, GLA_DK_TOTAL), 0) >> 7
               == lax.broadcasted_iota(jnp.int32, (GLA_DV_TOTAL, GLA_DK_TOTAL), 1) >> 6)
    gn = gn_ref[...]

    for b in range(BATCH):
        vb = p_ref[b, :, OFF_V:OFF_V + GLA_DV_TOTAL].astype(bf16)
        qs = qs3[b]
        qstack = jnp.concatenate(
            [jnp.where(lane_head == hd, qs, 0.0) for hd in range(GLA_HEADS)], axis=0).astype(bf16)
        scores = _dot_nt(qstack, kt3[b])
        st_b = st_ref[b]
        o_inter = _dot_nt(qe3[b], st_b.astype(bf16))
        gb = p_ref[b, :, OFF_G:OFF_G + GLA_DV_TOTAL]
        for hd in range(GLA_HEADS):
            sc = jnp.where(causal, scores[hd * TC:(hd + 1) * TC, :], 0.0).astype(bf16)
            cols = slice(hd * GLA_DV, (hd + 1) * GLA_DV)
            o = _dot(sc, vb[:, cols]) + o_inter[:, cols]
            g_h = gb[:, cols]
            y_ref[b * TC:(b + 1) * TC, cols] = (_rms(o, gn) * jax.nn.silu(g_h)).astype(bf16)
        kv_t = _dot_tn(vb, kl3[b])
        st_ref[b] = st_b * decay3[b] + jnp.where(st_mask, kv_t, 0.0)

    out = _dot(y_ref[...], wout_ref[...])
    o_ref[...] = x_ref[...] + out.reshape(BATCH, TC, D_MODEL)


def _mixer(proj3, x3, prm, layer):
    n_t = SEQ // TC

    def lay(shape):
        return pl.BlockSpec((None,) + shape, lambda i: (layer,) + (0,) * len(shape))

    return pl.pallas_call(
        _mixer_kernel,
        grid=(n_t,),
        in_specs=[
            pl.BlockSpec((BATCH, TC, D_PROJ), lambda i: (0, i, 0)),
            pl.BlockSpec((BATCH, TC, D_MODEL), lambda i: (0, i, 0)),
            lay((LANES, GLA_DK_TOTAL)),
            lay((16, 256)),
            lay((1, GLA_DV)),
            lay((2, S5_FLAT)),
            lay((S5_WIDTH, 2 * S5_FLAT)),
            lay((2 * S5_FLAT, S5_WIDTH)),
            lay((S5_WIDTH, S5_WIDTH)),
            lay((RG_WIDTH, RG_WIDTH)),
            lay((RG_WIDTH, RG_WIDTH)),
            lay((D_MODEL, D_MODEL)),
        ],
        out_specs=pl.BlockSpec((BATCH, TC, D_MODEL), lambda i: (0, i, 0)),
        out_shape=jax.ShapeDtypeStruct((BATCH, SEQ, D_MODEL), f32),
        scratch_shapes=[
            pltpu.VMEM((BATCH, GLA_DV_TOTAL, GLA_DK_TOTAL), f32),
            pltpu.VMEM((2 * S5_FLAT // LANES, BATCH, LANES), f32),
            pltpu.VMEM((RG_WIDTH // LANES, BATCH, LANES), f32),
            pltpu.VMEM((BATCH, TC, RG_WIDTH), f32),
            pltpu.VMEM((2 * S5_FLAT // LANES, ROWS, LANES), f32),
            pltpu.VMEM((RG_WIDTH // LANES, ROWS, LANES), f32),
            pltpu.VMEM((RG_WIDTH // LANES, ROWS, LANES), f32),
            pltpu.VMEM((ROWS, D_MODEL), bf16),
        ],
        compiler_params=pltpu.CompilerParams(
            dimension_semantics=("arbitrary",), vmem_limit_bytes=VMEM_LIMIT),
        name="mixer",
    )(proj3, x3, prm["wgk"], prm["vec"], prm["gla_norm"], prm["lam_bar"], prm["bd"], prm["cd"],
      prm["w_glu"], prm["wa"], prm["wx"], prm["w_out"])


def _ffn_kernel(x_ref, g_ref, wup_ref, cw_ref, cb_ref, wdn_ref, o_ref, carry_ref):
    @pl.when(pl.program_id(1) == 0)
    def _():
        carry_ref[...] = jnp.zeros_like(carry_ref)

    x = x_ref[...]
    h = _rms(x, g_ref[...]).astype(bf16)
    row = lax.broadcasted_iota(jnp.int32, (TM, FF_CHUNK), 0)
    acc = x
    for j in range(D_FF // FF_CHUNK):
        cols = slice(j * FF_CHUNK, (j + 1) * FF_CHUNK)
        up = _dot(h, wup_ref[:, cols])
        gv = _dot(h, wup_ref[:, D_FF + j * FF_CHUNK:D_FF + (j + 1) * FF_CHUNK])
        conv = up * cw_ref[FFN_CONV - 1:FFN_CONV, cols] + cb_ref[:, cols]
        for s in range(1, FFN_CONV):
            shifted = pltpu.roll(up, s, 0)
            for t in range(s):
                shifted = jnp.where(row == t, carry_ref[SUBLANES - s + t:SUBLANES - s + t + 1, cols], shifted)
            conv = conv + shifted * cw_ref[FFN_CONV - 1 - s:FFN_CONV - s, cols]
        carry_ref[:, cols] = up[TM - SUBLANES:, :]
        act = (jax.nn.gelu(conv) * gv).astype(bf16)
        acc = acc + _dot(act, wdn_ref[cols, :])
    o_ref[...] = acc


def _ffn(x3, mlp_norm, w_up, conv_w, conv_b, w_down, layer):
    def lay(shape):
        return pl.BlockSpec((None,) + shape, lambda b, i: (layer,) + (0,) * len(shape))

    return pl.pallas_call(
        _ffn_kernel,
        grid=(BATCH, SEQ // TM),
        in_specs=[
            pl.BlockSpec((None, TM, D_MODEL), lambda b, i: (b, i, 0)),
            lay((1, D_MODEL)),
            lay((D_MODEL, 2 * D_FF)),
            lay((FFN_CONV, D_FF)),
            lay((1, D_FF)),
            lay((D_FF, D_MODEL)),
        ],
        out_specs=pl.BlockSpec((None, TM, D_MODEL), lambda b, i: (b, i, 0)),
        out_shape=jax.ShapeDtypeStruct((BATCH, SEQ, D_MODEL), f32),
        scratch_shapes=[pltpu.VMEM((SUBLANES, D_FF), f32)],
        compiler_params=pltpu.CompilerParams(
            dimension_semantics=("arbitrary", "arbitrary"), vmem_limit_bytes=VMEM_LIMIT),
        name="ffn",
    )(x3, mlp_norm, w_up, conv_w, conv_b, w_down)


def _norm_kernel(x_ref, g_ref, o_ref):
    o_ref[...] = _rms(x_ref[...], g_ref[...])


def _final_norm(x2, g):
    rows = x2.shape[0]
    return pl.pallas_call(
        _norm_kernel,
        grid=(rows // TM,),
        in_specs=[pl.BlockSpec((TM, D_MODEL), lambda i: (i, 0)),
                  pl.BlockSpec((1, D_MODEL), lambda i: (0, 0))],
        out_specs=pl.BlockSpec((TM, D_MODEL), lambda i: (i, 0)),
        out_shape=jax.ShapeDtypeStruct((rows, D_MODEL), f32),
        compiler_params=pltpu.CompilerParams(dimension_semantics=("arbitrary",)),
        name="final_norm",
    )(x2, g)


def _block_diag(w):
    nl, nh, n, m = w.shape
    eye = jnp.eye(nh, dtype=w.dtype)
    return jnp.einsum('lhnm,hk->lhnkm', w, eye).reshape(nl, nh * n, nh * m)


def _prepare(w_in, gla_w_gk_up, gla_b_gk, gla_norm, s5_lambda_re, s5_lambda_im, s5_log_dt,
             s5_b_re, s5_b_im, s5_c_re, s5_c_im, s5_d, s5_w_glu, s5_b_glu, s5_norm,
             rg_conv_w, rg_conv_b, rg_w_a, rg_b_a, rg_w_x, rg_b_x, rg_lambda, rg_norm, w_out):
    nl = w_in.shape[0]
    gkl0 = 2 * GLA_DK_TOTAL + 2 * GLA_DV_TOTAL
    w_in_r = jnp.concatenate(
        [w_in[..., :gkl0], w_in[..., gkl0 + GLA_LOW_RANK:], w_in[..., gkl0:gkl0 + GLA_LOW_RANK],
         jnp.zeros((nl, D_MODEL, LANES - GLA_LOW_RANK), f32)], axis=-1).astype(bf16)
    wgk = jnp.concatenate(
        [gla_w_gk_up, jnp.zeros((nl, LANES - GLA_LOW_RANK, GLA_DK_TOTAL), f32)], axis=1)

    lam = lax.complex(s5_lambda_re, s5_lambda_im)
    dt = jnp.exp(s5_log_dt)[..., None]
    lam_bar = jnp.exp(lam * dt)
    b_bar = ((lam_bar - 1.0) / lam)[..., None] * lax.complex(s5_b_re, s5_b_im)
    bt = jnp.swapaxes(b_bar, 2, 3)
    bd = jnp.concatenate([_block_diag(bt.real), _block_diag(bt.imag)], axis=-1).astype(bf16)
    ct = jnp.swapaxes(lax.complex(s5_c_re, s5_c_im), 2, 3)
    cd = jnp.concatenate([_block_diag(ct.real), -_block_diag(ct.imag)], axis=1).astype(bf16)
    lam_flat = jnp.stack([lam_bar.real.reshape(nl, S5_FLAT), lam_bar.imag.reshape(nl, S5_FLAT)], axis=1)

    rows = [gla_b_gk, s5_d, s5_b_glu, s5_norm,
            rg_conv_w[:, 0], rg_conv_w[:, 1], rg_conv_w[:, 2], rg_conv_w[:, 3],
            rg_conv_b, rg_b_a, rg_b_x, rg_lambda, rg_norm]
    vec = jnp.stack(rows + [jnp.zeros_like(s5_d)] * (16 - len(rows)), axis=1)
    return dict(
        w_in=w_in_r, wgk=wgk, vec=vec, gla_norm=gla_norm[:, None, :], lam_bar=lam_flat,
        bd=bd, cd=cd, w_glu=s5_w_glu.astype(bf16),
        wa=_block_diag(rg_w_a).astype(bf16), wx=_block_diag(rg_w_x).astype(bf16),
        w_out=w_out.astype(bf16))


def kernel(x, attn_norm, w_in, gla_w_gk_up, gla_b_gk, gla_norm, s5_lambda_re, s5_lambda_im, s5_log_dt, s5_b_re, s5_b_im, s5_c_re, s5_c_im, s5_d, s5_w_glu, s5_b_glu, s5_norm, rg_conv_w, rg_conv_b, rg_w_a, rg_b_a, rg_w_x, rg_b_x, rg_lambda, rg_norm, w_out, mlp_norm, w_up, mlp_conv_w, mlp_conv_b, w_down, final_norm):
    prm = _prepare(w_in, gla_w_gk_up, gla_b_gk, gla_norm, s5_lambda_re, s5_lambda_im, s5_log_dt,
                   s5_b_re, s5_b_im, s5_c_re, s5_c_im, s5_d, s5_w_glu, s5_b_glu, s5_norm,
                   rg_conv_w, rg_conv_b, rg_w_a, rg_b_a, rg_w_x, rg_b_x, rg_lambda, rg_norm, w_out)
    attn_g = attn_norm[:, None, :]
    mlp_g = mlp_norm[:, None, :]
    w_up_b = w_up.astype(bf16)
    w_down_b = w_down.astype(bf16)
    conv_b = mlp_conv_b[:, None, :]
    for layer in range(DEPTH):
        proj = _inproj(x.reshape(BATCH * SEQ, D_MODEL), attn_g, prm["w_in"], layer)
        x = _mixer(proj.reshape(BATCH, SEQ, D_PROJ), x, prm, layer)
        x = _ffn(x, mlp_g, w_up_b, mlp_conv_w, conv_b, w_down_b, layer)
    out = _final_norm(x.reshape(BATCH * SEQ, D_MODEL), final_norm[None, :])
    return out.reshape(BATCH, SEQ, D_MODEL)
```

```python
import jax
import jax.numpy as jnp
from jax import lax
from jax.experimental import pallas as pl
from jax.experimental.pallas import tpu as pltpu

D_MODEL = 1024
BATCH = 8
SEQ = 2048
DEPTH = 4
EPS = 1e-6
GLA_HEADS = 4
GLA_DK_TOTAL = 256
GLA_DV_TOTAL = 512
GLA_DK = 64
GLA_DV = 128
GLA_LOW_RANK = 16
GLA_GATE_NORM = 16.0
GLA_CHUNK = 64
S5_WIDTH = 256
S5_GROUP = 16
S5_GROUPS = 16
S5_STATE = 64
S5_FLAT = S5_GROUPS * S5_STATE
RG_WIDTH = 256
RG_HEADS = 4
RG_HEAD_DIM = 64
RG_CONV = 4
RG_C = 8.0
D_FF = 2816
FFN_CONV = 3

LANES = 128
SUBLANES = 8
VMEM_LIMIT = 56 * 1024 * 1024

OFF_Q = 0
OFF_K = 256
OFF_V = 512
OFF_G = 1024
OFF_S5 = 1536
OFF_RGX = 1792
OFF_RGG = 2048
OFF_GKL = 2304
D_PROJ = 2432

TC = GLA_CHUNK
PITCH = TC + SUBLANES
ROWS = BATCH * TC
TM = 512
FF_CHUNK = 256

V_BGK, V_S5D, V_BGLU, V_S5N, V_CW0, V_CB, V_BA, V_BX, V_LAM, V_RGN = 0, 1, 2, 3, 4, 8, 9, 10, 11, 12

f32 = jnp.float32
bf16 = jnp.bfloat16


def _rms(x, g):
    return x * lax.rsqrt(jnp.mean(x * x, axis=-1, keepdims=True) + EPS) * g


def _softplus(x):
    return jnp.maximum(x, 0.0) + jnp.log1p(jnp.exp(-jnp.abs(x)))


def _dot(a, b):
    return jnp.dot(a, b, preferred_element_type=f32)


def _dot_nt(a, b):
    return lax.dot_general(a, b, (((1,), (1,)), ((), ())), preferred_element_type=f32)


def _dot_tn(a, b):
    return lax.dot_general(a, b, (((0,), (0,)), ((), ())), preferred_element_type=f32)


def _put_tiles(ref, val):
    for j in range(ref.shape[0]):
        for b in range(BATCH):
            ref[j, b * PITCH:b * PITCH + TC, :] = val[b * TC:(b + 1) * TC, j * LANES:(j + 1) * LANES]


def _get_tiles(ref):
    return jnp.concatenate(
        [jnp.concatenate([ref[j, b * PITCH:b * PITCH + TC, :] for j in range(ref.shape[0])], axis=-1)
         for b in range(BATCH)], axis=0)


def _split(x):
    hi = x.astype(bf16)
    return hi, (x - hi.astype(f32)).astype(bf16)


def _mixer_kernel(x_ref, ng_ref, win_ref, wgk_ref, vec_ref, gn_ref, lam_ref, bd_ref, cd_ref,
                  wglu_ref, wa_ref, wx_ref, wout_ref, o_ref,
                  p_ref, st_ref, s5_ref, rgh_ref, rgprev_ref, bu_ref, a_ref, xin_ref, y_ref):
    step = pl.program_id(0)
    hin = _rms(x_ref[...].reshape(ROWS, D_MODEL), ng_ref[...]).astype(bf16)
    p_ref[...] = _dot(hin, win_ref[...])

    @pl.when(step == 0)
    def _():
        st_ref[...] = jnp.zeros_like(st_ref)
        s5_ref[...] = jnp.zeros_like(s5_ref)
        rgh_ref[...] = jnp.zeros_like(rgh_ref)
        rgprev_ref[...] = jnp.zeros_like(rgprev_ref)

    def vec(r):
        return vec_ref[r:r + 1, :]

    def seg(off, width):
        return p_ref[:, off:off + width]

    def seg3(off, width):
        return p_ref[:, off:off + width].reshape(BATCH, TC, width)

    u = seg(OFF_S5, S5_WIDTH)
    _put_tiles(bu_ref, _dot(u.astype(bf16), bd_ref[...]))

    rx3 = seg3(OFF_RGX, RG_WIDTH)
    prev3 = rgprev_ref[...]
    t_idx = lax.broadcasted_iota(jnp.int32, (BATCH, TC, RG_WIDTH), 1)
    xc3 = rx3 * vec(V_CW0 + RG_CONV - 1) + vec(V_CB)
    for s in range(1, RG_CONV):
        shifted = jnp.where(t_idx < s, pltpu.roll(prev3, s, 1), pltpu.roll(rx3, s, 1))
        xc3 = xc3 + shifted * vec(V_CW0 + RG_CONV - 1 - s)
    rgprev_ref[...] = rx3
    xc = xc3.reshape(ROWS, RG_WIDTH)
    xcb = xc.astype(bf16)
    r = jax.nn.sigmoid(_dot(xcb, wa_ref[...]) + vec(V_BA))
    i = jax.nn.sigmoid(_dot(xcb, wx_ref[...]) + vec(V_BX))
    log_a = -RG_C * r * _softplus(-vec(V_LAM))
    a = jnp.exp(log_a)
    _put_tiles(a_ref, a)
    _put_tiles(xin_ref, jnp.sqrt(-jnp.tanh(log_a) * (a * a + 1.0)) * (i * xc))

    n_s5 = S5_FLAT // LANES
    n_rg = RG_WIDTH // LANES
    lre = [jnp.broadcast_to(lam_ref[0:1, j * LANES:(j + 1) * LANES], (BATCH, LANES)) for j in range(n_s5)]
    lim = [jnp.broadcast_to(lam_ref[1:2, j * LANES:(j + 1) * LANES], (BATCH, LANES)) for j in range(n_s5)]

    def scan_step(t, carry):
        sre, sim, h = carry
        rows = pl.ds(t, BATCH, stride=PITCH)
        nre, nim, nh = [], [], []
        for j in range(n_s5):
            re = lre[j] * sre[j] - lim[j] * sim[j] + bu_ref[j, rows, :]
            im = lre[j] * sim[j] + lim[j] * sre[j] + bu_ref[n_s5 + j, rows, :]
            bu_ref[j, rows, :] = re
            bu_ref[n_s5 + j, rows, :] = im
            nre.append(re)
            nim.append(im)
        for j in range(n_rg):
            hj = a_ref[j, rows, :] * h[j] + xin_ref[j, rows, :]
            xin_ref[j, rows, :] = hj
            nh.append(hj)
        return tuple(nre), tuple(nim), tuple(nh)

    init = (tuple(s5_ref[j] for j in range(n_s5)),
            tuple(s5_ref[n_s5 + j] for j in range(n_s5)),
            tuple(rgh_ref[j] for j in range(n_rg)))
    sre, sim, h = lax.fori_loop(0, TC, scan_step, init, unroll=2)
    for j in range(n_s5):
        s5_ref[j] = sre[j]
        s5_ref[n_s5 + j] = sim[j]
    for j in range(n_rg):
        rgh_ref[j] = h[j]

    ys = _dot(_get_tiles(bu_ref).astype(bf16), cd_ref[...]) + vec(V_S5D) * u
    ys = jax.nn.gelu(ys)
    ys = ys * jax.nn.sigmoid(_dot(ys.astype(bf16), wglu_ref[...]) + vec(V_BGLU))
    y_ref[:, GLA_DV_TOTAL:GLA_DV_TOTAL + S5_WIDTH] = _rms(ys, vec(V_S5N)).astype(bf16)

    yr = _get_tiles(xin_ref) * jax.nn.gelu(seg(OFF_RGG, RG_WIDTH))
    y_ref[:, GLA_DV_TOTAL + S5_WIDTH:] = _rms(yr, vec(V_RGN)).astype(bf16)

    gl_hi, gl_lo = _split(seg(OFF_GKL, LANES))
    wg_hi, wg_lo = _split(wgk_ref[...])
    gk_pre = _dot(gl_hi, wg_hi) + _dot(gl_lo, wg_hi) + _dot(gl_hi, wg_lo) + vec(V_BGK)
    gk = (jnp.minimum(gk_pre, 0.0) - jnp.log1p(jnp.exp(-jnp.abs(gk_pre)))) / GLA_GATE_NORM
    ri = lax.broadcasted_iota(jnp.int32, (ROWS, ROWS), 0)
    ci = lax.broadcasted_iota(jnp.int32, (ROWS, ROWS), 1)
    tril = jnp.where((ri - ci >= 0) & (ri - ci <= (ri & (TC - 1))), 1.0, 0.0).astype(bf16)
    gk_hi, gk_lo = _split(gk)
    bc = _dot(tril, gk_hi) + _dot(tril, gk_lo)
    bc3 = bc.reshape(BATCH, TC, GLA_DK_TOTAL)
    bmid = bc3[:, TC // 2 - 1:TC // 2, :]
    blast = bc3[:, TC - 1:TC, :]
    q3 = seg3(OFF_Q, GLA_DK_TOTAL) * (GLA_DK ** -0.5)
    k3 = seg3(OFF_K, GLA_DK_TOTAL)
    qs3 = q3 * jnp.exp(bc3 - bmid)
    kt3 = (k3 * jnp.exp(bmid - bc3)).astype(bf16)
    qe3 = (q3 * jnp.exp(bc3)).astype(bf16)
    kl3 = (k3 * jnp.exp(blast - bc3)).astype(bf16)
    decay3 = jnp.exp(blast)

    lane_head = lax.broadcasted_iota(jnp.int32, (TC, GLA_DK_TOTAL), 1) >> 6
    causal = (lax.broadcasted_iota(jnp.int32, (TC, TC), 1)
              <= lax.broadcasted_iota(jnp.int32, (TC, TC), 0))
    st_mask = (lax.broadcasted_iota(jnp.int32, (GLA_DV_TOTAL, GLA_DK_TOTAL), 0) >> 7
               == lax.broadcasted_iota(jnp.int32, (GLA_DV_TOTAL, GLA_DK_TOTAL), 1) >> 6)
    gn = gn_ref[...]

    for b in range(BATCH):
        vb = p_ref[b * TC:(b + 1) * TC, OFF_V:OFF_V + GLA_DV_TOTAL].astype(bf16)
        qs = qs3[b]
        qstack = jnp.concatenate(
            [jnp.where(lane_head == hd, qs, 0.0) for hd in range(GLA_HEADS)], axis=0).astype(bf16)
        scores = _dot_nt(qstack, kt3[b])
        st_b = st_ref[b]
        o_inter = _dot_nt(qe3[b], st_b.astype(bf16))
        gb = p_ref[b * TC:(b + 1) * TC, OFF_G:OFF_G + GLA_DV_TOTAL]
        for hd in range(GLA_HEADS):
            sc = jnp.where(causal, scores[hd * TC:(hd + 1) * TC, :], 0.0).astype(bf16)
            cols = slice(hd * GLA_DV, (hd + 1) * GLA_DV)
            o = _dot(sc, vb[:, cols]) + o_inter[:, cols]
            g_h = gb[:, cols]
            y_ref[b * TC:(b + 1) * TC, cols] = (_rms(o, gn) * jax.nn.silu(g_h)).astype(bf16)
        kv_t = _dot_tn(vb, kl3[b])
        st_ref[b] = st_b * decay3[b] + jnp.where(st_mask, kv_t, 0.0)

    out = _dot(y_ref[...], wout_ref[...])
    o_ref[...] = x_ref[...] + out.reshape(BATCH, TC, D_MODEL)


def _mixer(x3, attn_g, prm, layer):
    n_t = SEQ // TC

    def lay(shape):
        return pl.BlockSpec((None,) + shape, lambda i: (layer,) + (0,) * len(shape))

    return pl.pallas_call(
        _mixer_kernel,
        grid=(n_t,),
        in_specs=[
            pl.BlockSpec((BATCH, TC, D_MODEL), lambda i: (0, i, 0)),
            lay((1, D_MODEL)),
            lay((D_MODEL, D_PROJ)),
            lay((LANES, GLA_DK_TOTAL)),
            lay((16, 256)),
            lay((1, GLA_DV)),
            lay((2, S5_FLAT)),
            lay((S5_WIDTH, 2 * S5_FLAT)),
            lay((2 * S5_FLAT, S5_WIDTH)),
            lay((S5_WIDTH, S5_WIDTH)),
            lay((RG_WIDTH, RG_WIDTH)),
            lay((RG_WIDTH, RG_WIDTH)),
            lay((D_MODEL, D_MODEL)),
        ],
        out_specs=pl.BlockSpec((BATCH, TC, D_MODEL), lambda i: (0, i, 0)),
        out_shape=jax.ShapeDtypeStruct((BATCH, SEQ, D_MODEL), f32),
        scratch_shapes=[
            pltpu.VMEM((ROWS, D_PROJ), f32),
            pltpu.VMEM((BATCH, GLA_DV_TOTAL, GLA_DK_TOTAL), f32),
            pltpu.VMEM((2 * S5_FLAT // LANES, BATCH, LANES), f32),
            pltpu.VMEM((RG_WIDTH // LANES, BATCH, LANES), f32),
            pltpu.VMEM((BATCH, TC, RG_WIDTH), f32),
            pltpu.VMEM((2 * S5_FLAT // LANES, BATCH * PITCH, LANES), f32),
            pltpu.VMEM((RG_WIDTH // LANES, BATCH * PITCH, LANES), f32),
            pltpu.VMEM((RG_WIDTH // LANES, BATCH * PITCH, LANES), f32),
            pltpu.VMEM((ROWS, D_MODEL), bf16),
        ],
        compiler_params=pltpu.CompilerParams(
            dimension_semantics=("arbitrary",), vmem_limit_bytes=VMEM_LIMIT),
        name="mixer",
    )(x3, attn_g, prm["w_in"], prm["wgk"], prm["vec"], prm["gla_norm"], prm["lam_bar"], prm["bd"], prm["cd"],
      prm["w_glu"], prm["wa"], prm["wx"], prm["w_out"])


def _ffn_kernel(x_ref, g_ref, wup_ref, cw_ref, cb_ref, wdn_ref, o_ref, carry_ref, act_ref):
    @pl.when(pl.program_id(1) == 0)
    def _():
        carry_ref[...] = jnp.zeros_like(carry_ref)

    x = x_ref[...]
    h = _rms(x, g_ref[...]).astype(bf16)
    row = lax.broadcasted_iota(jnp.int32, (TM, FF_CHUNK), 0)
    for j in range(D_FF // FF_CHUNK):
        cols = slice(j * FF_CHUNK, (j + 1) * FF_CHUNK)
        up = _dot(h, wup_ref[:, cols])
        gv = _dot(h, wup_ref[:, D_FF + j * FF_CHUNK:D_FF + (j + 1) * FF_CHUNK])
        conv = up * cw_ref[FFN_CONV - 1:FFN_CONV, cols] + cb_ref[:, cols]
        for s in range(1, FFN_CONV):
            shifted = pltpu.roll(up, s, 0)
            for t in range(s):
                shifted = jnp.where(row == t, carry_ref[SUBLANES - s + t:SUBLANES - s + t + 1, cols], shifted)
            conv = conv + shifted * cw_ref[FFN_CONV - 1 - s:FFN_CONV - s, cols]
        carry_ref[:, cols] = up[TM - SUBLANES:, :]
        act_ref[:, cols] = (jax.nn.gelu(conv) * gv).astype(bf16)
    o_ref[...] = x + _dot(act_ref[...], wdn_ref[...])


def _ffn(x3, mlp_norm, w_up, conv_w, conv_b, w_down, layer):
    def lay(shape):
        return pl.BlockSpec((None,) + shape, lambda b, i: (layer,) + (0,) * len(shape))

    return pl.pallas_call(
        _ffn_kernel,
        grid=(BATCH, SEQ // TM),
        in_specs=[
            pl.BlockSpec((None, TM, D_MODEL), lambda b, i: (b, i, 0)),
            lay((1, D_MODEL)),
            lay((D_MODEL, 2 * D_FF)),
            lay((FFN_CONV, D_FF)),
            lay((1, D_FF)),
            lay((D_FF, D_MODEL)),
        ],
        out_specs=pl.BlockSpec((None, TM, D_MODEL), lambda b, i: (b, i, 0)),
        out_shape=jax.ShapeDtypeStruct((BATCH, SEQ, D_MODEL), f32),
        scratch_shapes=[pltpu.VMEM((SUBLANES, D_FF), f32),
                        pltpu.VMEM((TM, D_FF), bf16)],
        compiler_params=pltpu.CompilerParams(
            dimension_semantics=("arbitrary", "arbitrary"), vmem_limit_bytes=VMEM_LIMIT),
        name="ffn",
    )(x3, mlp_norm, w_up, conv_w, conv_b, w_down)


def _norm_kernel(x_ref, g_ref, o_ref):
    o_ref[...] = _rms(x_ref[...], g_ref[...])


def _final_norm(x2, g):
    rows = x2.shape[0]
    return pl.pallas_call(
        _norm_kernel,
        grid=(rows // TM,),
        in_specs=[pl.BlockSpec((TM, D_MODEL), lambda i: (i, 0)),
                  pl.BlockSpec((1, D_MODEL), lambda i: (0, 0))],
        out_specs=pl.BlockSpec((TM, D_MODEL), lambda i: (i, 0)),
        out_shape=jax.ShapeDtypeStruct((rows, D_MODEL), f32),
        compiler_params=pltpu.CompilerParams(dimension_semantics=("arbitrary",)),
        name="final_norm",
    )(x2, g)


def _block_diag(w):
    nl, nh, n, m = w.shape
    eye = jnp.eye(nh, dtype=w.dtype)
    return jnp.einsum('lhnm,hk->lhnkm', w, eye).reshape(nl, nh * n, nh * m)


def _prepare(w_in, gla_w_gk_up, gla_b_gk, gla_norm, s5_lambda_re, s5_lambda_im, s5_log_dt,
             s5_b_re, s5_b_im, s5_c_re, s5_c_im, s5_d, s5_w_glu, s5_b_glu, s5_norm,
             rg_conv_w, rg_conv_b, rg_w_a, rg_b_a, rg_w_x, rg_b_x, rg_lambda, rg_norm, w_out):
    nl = w_in.shape[0]
    gkl0 = 2 * GLA_DK_TOTAL + 2 * GLA_DV_TOTAL
    w_in_r = jnp.concatenate(
        [w_in[..., :gkl0], w_in[..., gkl0 + GLA_LOW_RANK:], w_in[..., gkl0:gkl0 + GLA_LOW_RANK],
         jnp.zeros((nl, D_MODEL, LANES - GLA_LOW_RANK), f32)], axis=-1).astype(bf16)
    wgk = jnp.concatenate(
        [gla_w_gk_up, jnp.zeros((nl, LANES - GLA_LOW_RANK, GLA_DK_TOTAL), f32)], axis=1)

    lam = lax.complex(s5_lambda_re, s5_lambda_im)
    dt = jnp.exp(s5_log_dt)[..., None]
    lam_bar = jnp.exp(lam * dt)
    b_bar = ((lam_bar - 1.0) / lam)[..., None] * lax.complex(s5_b_re, s5_b_im)
    bt = jnp.swapaxes(b_bar, 2, 3)
    bd = jnp.concatenate([_block_diag(bt.real), _block_diag(bt.imag)], axis=-1).astype(bf16)
    ct = jnp.swapaxes(lax.complex(s5_c_re, s5_c_im), 2, 3)
    cd = jnp.concatenate([_block_diag(ct.real), -_block_diag(ct.imag)], axis=1).astype(bf16)
    lam_flat = jnp.stack([lam_bar.real.reshape(nl, S5_FLAT), lam_bar.imag.reshape(nl, S5_FLAT)], axis=1)

    rows = [gla_b_gk, s5_d, s5_b_glu, s5_norm,
            rg_conv_w[:, 0], rg_conv_w[:, 1], rg_conv_w[:, 2], rg_conv_w[:, 3],
            rg_conv_b, rg_b_a, rg_b_x, rg_lambda, rg_norm]
    vec = jnp.stack(rows + [jnp.zeros_like(s5_d)] * (16 - len(rows)), axis=1)
    return dict(
        w_in=w_in_r, wgk=wgk, vec=vec, gla_norm=gla_norm[:, None, :], lam_bar=lam_flat,
        bd=bd, cd=cd, w_glu=s5_w_glu.astype(bf16),
        wa=_block_diag(rg_w_a).astype(bf16), wx=_block_diag(rg_w_x).astype(bf16),
        w_out=w_out.astype(bf16))


def kernel(x, attn_norm, w_in, gla_w_gk_up, gla_b_gk, gla_norm, s5_lambda_re, s5_lambda_im, s5_log_dt, s5_b_re, s5_b_im, s5_c_re, s5_c_im, s5_d, s5_w_glu, s5_b_glu, s5_norm, rg_conv_w, rg_conv_b, rg_w_a, rg_b_a, rg_w_x, rg_b_x, rg_lambda, rg_norm, w_out, mlp_norm, w_up, mlp_conv_w, mlp_conv_b, w_down, final_norm):
    prm = _prepare(w_in, gla_w_gk_up, gla_b_gk, gla_norm, s5_lambda_re, s5_lambda_im, s5_log_dt,
                   s5_b_re, s5_b_im, s5_c_re, s5_c_im, s5_d, s5_w_glu, s5_b_glu, s5_norm,
                   rg_conv_w, rg_conv_b, rg_w_a, rg_b_a, rg_w_x, rg_b_x, rg_lambda, rg_norm, w_out)
    attn_g = attn_norm[:, None, :]
    mlp_g = mlp_norm[:, None, :]
    w_up_b = w_up.astype(bf16)
    w_down_b = w_down.astype(bf16)
    conv_b = mlp_conv_b[:, None, :]
    for layer in range(DEPTH):
        x = _mixer(x, attn_g, prm, layer)
        x = _ffn(x, mlp_g, w_up_b, mlp_conv_w, conv_b, w_down_b, layer)
    out = _final_norm(x.reshape(BATCH * SEQ, D_MODEL), final_norm[None, :])
    return out.reshape(BATCH, SEQ, D_MODEL)
```

```python
import functools

import jax
import jax.numpy as jnp
from jax import lax
from jax.experimental import pallas as pl
from jax.experimental.pallas import tpu as pltpu

D_MODEL = 1024
BATCH = 8
SEQ = 2048
DEPTH = 4
EPS = 1e-6
GLA_HEADS = 4
GLA_DK_TOTAL = 256
GLA_DV_TOTAL = 512
GLA_DK = 64
GLA_DV = 128
GLA_LOW_RANK = 16
GLA_GATE_NORM = 16.0
GLA_CHUNK = 64
S5_WIDTH = 256
S5_GROUP = 16
S5_GROUPS = 16
S5_STATE = 64
S5_FLAT = S5_GROUPS * S5_STATE
RG_WIDTH = 256
RG_HEADS = 4
RG_HEAD_DIM = 64
RG_CONV = 4
RG_C = 8.0
D_FF = 2816
FFN_CONV = 3

LANES = 128
SUBLANES = 8
VMEM_LIMIT = 56 * 1024 * 1024

OFF_Q = 0
OFF_K = 256
OFF_V = 512
OFF_G = 1024
OFF_S5 = 1536
OFF_RGX = 1792
OFF_RGG = 2048
OFF_GKL = 2304
D_PROJ = 2432

TC = GLA_CHUNK
PITCH = TC + SUBLANES
ROWS = BATCH * TC
TM = 512
FF_CHUNK = 256
SCAN_UNROLL = TC

V_BGK, V_S5D, V_BGLU, V_S5N, V_CW0, V_CB, V_BA, V_BX, V_LAM, V_RGN = 0, 1, 2, 3, 4, 8, 9, 10, 11, 12

f32 = jnp.float32
bf16 = jnp.bfloat16


def _rms(x, g):
    return x * lax.rsqrt(jnp.mean(x * x, axis=-1, keepdims=True) + EPS) * g


def _softplus(x):
    return jnp.maximum(x, 0.0) + jnp.log1p(jnp.exp(-jnp.abs(x)))


def _dot(a, b):
    return jnp.dot(a, b, preferred_element_type=f32)


def _dot_nt(a, b):
    return lax.dot_general(a, b, (((1,), (1,)), ((), ())), preferred_element_type=f32)


def _dot_tn(a, b):
    return lax.dot_general(a, b, (((0,), (0,)), ((), ())), preferred_element_type=f32)


def _to_time_major(ref, val):
    n = ref.shape[0]
    for j in range(n):
        for b in range(BATCH):
            ref[j, b * PITCH:b * PITCH + TC, :] = val[b * TC:(b + 1) * TC, j * LANES:(j + 1) * LANES]
    return jnp.concatenate(
        [jnp.concatenate([ref[j, pl.ds(t, BATCH, stride=PITCH), :] for j in range(n)], axis=-1)
         for t in range(TC)], axis=0)


def _to_batch_major(ref, val):
    n = ref.shape[0]
    for j in range(n):
        for t in range(TC):
            ref[j, pl.ds(t, BATCH, stride=PITCH), :] = val[t * BATCH:(t + 1) * BATCH, j * LANES:(j + 1) * LANES]
    return jnp.concatenate(
        [jnp.concatenate([ref[j, b * PITCH:b * PITCH + TC, :] for j in range(n)], axis=-1)
         for b in range(BATCH)], axis=0)


def _split(x):
    hi = x.astype(bf16)
    return hi, (x - hi.astype(f32)).astype(bf16)


def _mixer_kernel(x_ref, ng_ref, win_ref, wgk_ref, vec_ref, gn_ref, lam_ref, bd_ref, cd_ref,
                  wglu_ref, wa_ref, wx_ref, wout_ref, o_ref,
                  p_ref, st_ref, s5_ref, rgh_ref, rgprev_ref, bu_ref, a_ref, xin_ref, y_ref,
                  pu_ref, px_ref, pg_ref, pys_ref, pyr_ref):
    step = pl.program_id(0)
    hin = _rms(x_ref[...].reshape(ROWS, D_MODEL), ng_ref[...]).astype(bf16)
    p_ref[...] = _dot(hin, win_ref[...])

    @pl.when(step == 0)
    def _():
        st_ref[...] = jnp.zeros_like(st_ref)
        s5_ref[...] = jnp.zeros_like(s5_ref)
        rgh_ref[...] = jnp.zeros_like(rgh_ref)
        rgprev_ref[...] = jnp.zeros_like(rgprev_ref)

    def vec(r):
        return vec_ref[r:r + 1, :]

    def seg(off, width):
        return p_ref[:, off:off + width]

    def seg3(off, width):
        return p_ref[:, off:off + width].reshape(BATCH, TC, width)

    u = _to_time_major(pu_ref, seg(OFF_S5, S5_WIDTH))
    bu_ref[...] = _dot(u.astype(bf16), bd_ref[...])

    rx = _to_time_major(px_ref, seg(OFF_RGX, RG_WIDTH))
    prev = rgprev_ref[...]
    xc = rx * vec(V_CW0 + RG_CONV - 1) + vec(V_CB)
    for s in range(1, RG_CONV):
        shifted = jnp.concatenate([prev[(RG_CONV - 1 - s) * BATCH:, :], rx[:ROWS - s * BATCH, :]], axis=0)
        xc = xc + shifted * vec(V_CW0 + RG_CONV - 1 - s)
    rgprev_ref[...] = rx[ROWS - (RG_CONV - 1) * BATCH:, :]
    xcb = xc.astype(bf16)
    r = jax.nn.sigmoid(_dot(xcb, wa_ref[...]) + vec(V_BA))
    i = jax.nn.sigmoid(_dot(xcb, wx_ref[...]) + vec(V_BX))
    log_a = -RG_C * r * _softplus(-vec(V_LAM))
    a = jnp.exp(log_a)
    a_ref[...] = a
    xin_ref[...] = jnp.sqrt(-jnp.tanh(log_a) * (a * a + 1.0)) * (i * xc)

    n_s5 = S5_FLAT // LANES
    n_rg = RG_WIDTH // LANES
    lre = [jnp.broadcast_to(lam_ref[0:1, j * LANES:(j + 1) * LANES], (BATCH, LANES)) for j in range(n_s5)]
    lim = [jnp.broadcast_to(lam_ref[1:2, j * LANES:(j + 1) * LANES], (BATCH, LANES)) for j in range(n_s5)]

    def scan_step(t, carry):
        sre, sim, h = carry
        rows = pl.ds(pl.multiple_of(t * BATCH, BATCH), BATCH)
        nre, nim, nh = [], [], []
        for j in range(n_s5):
            c_re = slice(j * LANES, (j + 1) * LANES)
            c_im = slice(S5_FLAT + j * LANES, S5_FLAT + (j + 1) * LANES)
            re = lre[j] * sre[j] - lim[j] * sim[j] + bu_ref[rows, c_re]
            im = lre[j] * sim[j] + lim[j] * sre[j] + bu_ref[rows, c_im]
            bu_ref[rows, c_re] = re
            bu_ref[rows, c_im] = im
            nre.append(re)
            nim.append(im)
        for j in range(n_rg):
            cj = slice(j * LANES, (j + 1) * LANES)
            hj = a_ref[rows, cj] * h[j] + xin_ref[rows, cj]
            xin_ref[rows, cj] = hj
            nh.append(hj)
        return tuple(nre), tuple(nim), tuple(nh)

    init = (tuple(s5_ref[j] for j in range(n_s5)),
            tuple(s5_ref[n_s5 + j] for j in range(n_s5)),
            tuple(rgh_ref[j] for j in range(n_rg)))
    sre, sim, h = lax.fori_loop(0, TC, scan_step, init, unroll=SCAN_UNROLL)
    for j in range(n_s5):
        s5_ref[j] = sre[j]
        s5_ref[n_s5 + j] = sim[j]
    for j in range(n_rg):
        rgh_ref[j] = h[j]

    ys = _dot(bu_ref[...].astype(bf16), cd_ref[...]) + vec(V_S5D) * u
    ys = jax.nn.gelu(ys)
    ys = ys * jax.nn.sigmoid(_dot(ys.astype(bf16), wglu_ref[...]) + vec(V_BGLU))
    y_ref[:, GLA_DV_TOTAL:GLA_DV_TOTAL + S5_WIDTH] = _to_batch_major(pys_ref, _rms(ys, vec(V_S5N))).astype(bf16)

    yr = xin_ref[...] * jax.nn.gelu(_to_time_major(pg_ref, seg(OFF_RGG, RG_WIDTH)))
    y_ref[:, GLA_DV_TOTAL + S5_WIDTH:] = _to_batch_major(pyr_ref, _rms(yr, vec(V_RGN))).astype(bf16)

    gl_hi, gl_lo = _split(seg(OFF_GKL, LANES))
    wg_hi, wg_lo = _split(wgk_ref[...])
    gk_pre = _dot(gl_hi, wg_hi) + _dot(gl_lo, wg_hi) + _dot(gl_hi, wg_lo) + vec(V_BGK)
    gk = (jnp.minimum(gk_pre, 0.0) - jnp.log1p(jnp.exp(-jnp.abs(gk_pre)))) / GLA_GATE_NORM
    ri = lax.broadcasted_iota(jnp.int32, (ROWS, ROWS), 0)
    ci = lax.broadcasted_iota(jnp.int32, (ROWS, ROWS), 1)
    tril = jnp.where((ri - ci >= 0) & (ri - ci <= (ri & (TC - 1))), 1.0, 0.0).astype(bf16)
    gk_hi, gk_lo = _split(gk)
    bc = _dot(tril, gk_hi) + _dot(tril, gk_lo)
    bc3 = bc.reshape(BATCH, TC, GLA_DK_TOTAL)
    bmid = bc3[:, TC // 2 - 1:TC // 2, :]
    blast = bc3[:, TC - 1:TC, :]
    q3 = seg3(OFF_Q, GLA_DK_TOTAL) * (GLA_DK ** -0.5)
    k3 = seg3(OFF_K, GLA_DK_TOTAL)
    qs3 = q3 * jnp.exp(bc3 - bmid)
    kt3 = (k3 * jnp.exp(bmid - bc3)).astype(bf16)
    qe3 = (q3 * jnp.exp(bc3)).astype(bf16)
    kl3 = (k3 * jnp.exp(blast - bc3)).astype(bf16)
    decay3 = jnp.exp(blast)

    lane_head = lax.broadcasted_iota(jnp.int32, (TC, GLA_DK_TOTAL), 1) >> 6
    causal = (lax.broadcasted_iota(jnp.int32, (TC, TC), 1)
              <= lax.broadcasted_iota(jnp.int32, (TC, TC), 0))
    st_mask = (lax.broadcasted_iota(jnp.int32, (GLA_DV_TOTAL, GLA_DK_TOTAL), 0) >> 7
               == lax.broadcasted_iota(jnp.int32, (GLA_DV_TOTAL, GLA_DK_TOTAL), 1) >> 6)
    gn = gn_ref[...]

    for b in range(BATCH):
        vb = p_ref[b * TC:(b + 1) * TC, OFF_V:OFF_V + GLA_DV_TOTAL].astype(bf16)
        qs = qs3[b]
        qstack = jnp.concatenate(
            [jnp.where(lane_head == hd, qs, 0.0) for hd in range(GLA_HEADS)], axis=0).astype(bf16)
        scores = _dot_nt(qstack, kt3[b])
        st_b = st_ref[b]
        o_inter = _dot_nt(qe3[b], st_b.astype(bf16))
        gb = p_ref[b * TC:(b + 1) * TC, OFF_G:OFF_G + GLA_DV_TOTAL]
        for hd in range(GLA_HEADS):
            sc = jnp.where(causal, scores[hd * TC:(hd + 1) * TC, :], 0.0).astype(bf16)
            cols = slice(hd * GLA_DV, (hd + 1) * GLA_DV)
            o = _dot(sc, vb[:, cols]) + o_inter[:, cols]
            g_h = gb[:, cols]
            y_ref[b * TC:(b + 1) * TC, cols] = (_rms(o, gn) * jax.nn.silu(g_h)).astype(bf16)
        kv_t = _dot_tn(vb, kl3[b])
        st_ref[b] = st_b * decay3[b] + jnp.where(st_mask, kv_t, 0.0)

    out = _dot(y_ref[...], wout_ref[...])
    o_ref[...] = x_ref[...] + out.reshape(BATCH, TC, D_MODEL)


def _mixer(x3, attn_g, prm, layer):
    n_t = SEQ // TC

    def lay(shape):
        return pl.BlockSpec((None,) + shape, lambda i: (layer,) + (0,) * len(shape))

    return pl.pallas_call(
        _mixer_kernel,
        grid=(n_t,),
        in_specs=[
            pl.BlockSpec((BATCH, TC, D_MODEL), lambda i: (0, i, 0)),
            lay((1, D_MODEL)),
            lay((D_MODEL, D_PROJ)),
            lay((LANES, GLA_DK_TOTAL)),
            lay((16, 256)),
            lay((1, GLA_DV)),
            lay((2, S5_FLAT)),
            lay((S5_WIDTH, 2 * S5_FLAT)),
            lay((2 * S5_FLAT, S5_WIDTH)),
            lay((S5_WIDTH, S5_WIDTH)),
            lay((RG_WIDTH, RG_WIDTH)),
            lay((RG_WIDTH, RG_WIDTH)),
            lay((D_MODEL, D_MODEL)),
        ],
        out_specs=pl.BlockSpec((BATCH, TC, D_MODEL), lambda i: (0, i, 0)),
        out_shape=jax.ShapeDtypeStruct((BATCH, SEQ, D_MODEL), f32),
        scratch_shapes=[
            pltpu.VMEM((ROWS, D_PROJ), f32),
            pltpu.VMEM((BATCH, GLA_DV_TOTAL, GLA_DK_TOTAL), f32),
            pltpu.VMEM((2 * S5_FLAT // LANES, BATCH, LANES), f32),
            pltpu.VMEM((RG_WIDTH // LANES, BATCH, LANES), f32),
            pltpu.VMEM(((RG_CONV - 1) * BATCH, RG_WIDTH), f32),
            pltpu.VMEM((ROWS, 2 * S5_FLAT), f32),
            pltpu.VMEM((ROWS, RG_WIDTH), f32),
            pltpu.VMEM((ROWS, RG_WIDTH), f32),
            pltpu.VMEM((ROWS, D_MODEL), bf16),
        ] + [pltpu.VMEM((S5_WIDTH // LANES, BATCH * PITCH, LANES), f32)] * 5,
        compiler_params=pltpu.CompilerParams(
            dimension_semantics=("arbitrary",), vmem_limit_bytes=VMEM_LIMIT),
        name="mixer",
    )(x3, attn_g, prm["w_in"], prm["wgk"], prm["vec"], prm["gla_norm"], prm["lam_bar"], prm["bd"], prm["cd"],
      prm["w_glu"], prm["wa"], prm["wx"], prm["w_out"])


def _ffn_kernel(x_ref, g_ref, wup_ref, cw_ref, cb_ref, wdn_ref, fg_ref, o_ref, carry_ref, act_ref, *, last):
    @pl.when(pl.program_id(1) == 0)
    def _():
        carry_ref[...] = jnp.zeros_like(carry_ref)

    x = x_ref[...]
    h = _rms(x, g_ref[...]).astype(bf16)
    row = lax.broadcasted_iota(jnp.int32, (TM, FF_CHUNK), 0)
    for j in range(D_FF // FF_CHUNK):
        cols = slice(j * FF_CHUNK, (j + 1) * FF_CHUNK)
        up = _dot(h, wup_ref[:, cols])
        gv = _dot(h, wup_ref[:, D_FF + j * FF_CHUNK:D_FF + (j + 1) * FF_CHUNK])
        conv = up * cw_ref[FFN_CONV - 1:FFN_CONV, cols] + cb_ref[:, cols]
        for s in range(1, FFN_CONV):
            shifted = pltpu.roll(up, s, 0)
            for t in range(s):
                shifted = jnp.where(row == t, carry_ref[SUBLANES - s + t:SUBLANES - s + t + 1, cols], shifted)
            conv = conv + shifted * cw_ref[FFN_CONV - 1 - s:FFN_CONV - s, cols]
        carry_ref[:, cols] = up[TM - SUBLANES:, :]
        act_ref[:, cols] = (jax.nn.gelu(conv) * gv).astype(bf16)
    out = x + _dot(act_ref[...], wdn_ref[...])
    o_ref[...] = _rms(out, fg_ref[...]) if last else out


def _ffn(x3, mlp_norm, w_up, conv_w, conv_b, w_down, final_g, layer):
    def lay(shape):
        return pl.BlockSpec((None,) + shape, lambda b, i: (layer,) + (0,) * len(shape))

    return pl.pallas_call(
        functools.partial(_ffn_kernel, last=(layer == DEPTH - 1)),
        grid=(BATCH, SEQ // TM),
        in_specs=[
            pl.BlockSpec((None, TM, D_MODEL), lambda b, i: (b, i, 0)),
            lay((1, D_MODEL)),
            lay((D_MODEL, 2 * D_FF)),
            lay((FFN_CONV, D_FF)),
            lay((1, D_FF)),
            lay((D_FF, D_MODEL)),
            pl.BlockSpec((1, D_MODEL), lambda b, i: (0, 0)),
        ],
        out_specs=pl.BlockSpec((None, TM, D_MODEL), lambda b, i: (b, i, 0)),
        out_shape=jax.ShapeDtypeStruct((BATCH, SEQ, D_MODEL), f32),
        scratch_shapes=[pltpu.VMEM((SUBLANES, D_FF), f32),
                        pltpu.VMEM((TM, D_FF), bf16)],
        compiler_params=pltpu.CompilerParams(
            dimension_semantics=("arbitrary", "arbitrary"), vmem_limit_bytes=VMEM_LIMIT),
        name="ffn",
    )(x3, mlp_norm, w_up, conv_w, conv_b, w_down, final_g)


def _block_diag(w):
    nl, nh, n, m = w.shape
    eye = jnp.eye(nh, dtype=w.dtype)
    return jnp.einsum('lhnm,hk->lhnkm', w, eye).reshape(nl, nh * n, nh * m)


def _prepare(w_in, gla_w_gk_up, gla_b_gk, gla_norm, s5_lambda_re, s5_lambda_im, s5_log_dt,
             s5_b_re, s5_b_im, s5_c_re, s5_c_im, s5_d, s5_w_glu, s5_b_glu, s5_norm,
             rg_conv_w, rg_conv_b, rg_w_a, rg_b_a, rg_w_x, rg_b_x, rg_lambda, rg_norm, w_out):
    nl = w_in.shape[0]
    gkl0 = 2 * GLA_DK_TOTAL + 2 * GLA_DV_TOTAL
    w_in_r = jnp.concatenate(
        [w_in[..., :gkl0], w_in[..., gkl0 + GLA_LOW_RANK:], w_in[..., gkl0:gkl0 + GLA_LOW_RANK],
         jnp.zeros((nl, D_MODEL, LANES - GLA_LOW_RANK), f32)], axis=-1).astype(bf16)
    wgk = jnp.concatenate(
        [gla_w_gk_up, jnp.zeros((nl, LANES - GLA_LOW_RANK, GLA_DK_TOTAL), f32)], axis=1)

    lam = lax.complex(s5_lambda_re, s5_lambda_im)
    dt = jnp.exp(s5_log_dt)[..., None]
    lam_bar = jnp.exp(lam * dt)
    b_bar = ((lam_bar - 1.0) / lam)[..., None] * lax.complex(s5_b_re, s5_b_im)
    bt = jnp.swapaxes(b_bar, 2, 3)
    bd = jnp.concatenate([_block_diag(bt.real), _block_diag(bt.imag)], axis=-1).astype(bf16)
    ct = jnp.swapaxes(lax.complex(s5_c_re, s5_c_im), 2, 3)
    cd = jnp.concatenate([_block_diag(ct.real), -_block_diag(ct.imag)], axis=1).astype(bf16)
    lam_flat = jnp.stack([lam_bar.real.reshape(nl, S5_FLAT), lam_bar.imag.reshape(nl, S5_FLAT)], axis=1)

    rows = [gla_b_gk, s5_d, s5_b_glu, s5_norm,
            rg_conv_w[:, 0], rg_conv_w[:, 1], rg_conv_w[:, 2], rg_conv_w[:, 3],
            rg_conv_b, rg_b_a, rg_b_x, rg_lambda, rg_norm]
    vec = jnp.stack(rows + [jnp.zeros_like(s5_d)] * (16 - len(rows)), axis=1)
    return dict(
        w_in=w_in_r, wgk=wgk, vec=vec, gla_norm=gla_norm[:, None, :], lam_bar=lam_flat,
        bd=bd, cd=cd, w_glu=s5_w_glu.astype(bf16),
        wa=_block_diag(rg_w_a).astype(bf16), wx=_block_diag(rg_w_x).astype(bf16),
        w_out=w_out.astype(bf16))


def kernel(x, attn_norm, w_in, gla_w_gk_up, gla_b_gk, gla_norm, s5_lambda_re, s5_lambda_im, s5_log_dt, s5_b_re, s5_b_im, s5_c_re, s5_c_im, s5_d, s5_w_glu, s5_b_glu, s5_norm, rg_conv_w, rg_conv_b, rg_w_a, rg_b_a, rg_w_x, rg_b_x, rg_lambda, rg_norm, w_out, mlp_norm, w_up, mlp_conv_w, mlp_conv_b, w_down, final_norm):
    prm = _prepare(w_in, gla_w_gk_up, gla_b_gk, gla_norm, s5_lambda_re, s5_lambda_im, s5_log_dt,
                   s5_b_re, s5_b_im, s5_c_re, s5_c_im, s5_d, s5_w_glu, s5_b_glu, s5_norm,
                   rg_conv_w, rg_conv_b, rg_w_a, rg_b_a, rg_w_x, rg_b_x, rg_lambda, rg_norm, w_out)
    attn_g = attn_norm[:, None, :]
    mlp_g = mlp_norm[:, None, :]
    w_up_b = w_up.astype(bf16)
    w_down_b = w_down.astype(bf16)
    conv_b = mlp_conv_b[:, None, :]
    for layer in range(DEPTH):
        x = _mixer(x, attn_g, prm, layer)
        x = _ffn(x, mlp_g, w_up_b, mlp_conv_w, conv_b, w_down_b, final_norm[None, :], layer)
    return x
```

```python
import functools

import jax
import jax.numpy as jnp
from jax import lax
from jax.experimental import pallas as pl
from jax.experimental.pallas import tpu as pltpu

D_MODEL = 1024
BATCH = 8
SEQ = 2048
DEPTH = 4
EPS = 1e-6
GLA_HEADS = 4
GLA_DK_TOTAL = 256
GLA_DV_TOTAL = 512
GLA_DK = 64
GLA_DV = 128
GLA_LOW_RANK = 16
GLA_GATE_NORM = 16.0
GLA_CHUNK = 64
S5_WIDTH = 256
S5_GROUP = 16
S5_GROUPS = 16
S5_STATE = 64
S5_FLAT = S5_GROUPS * S5_STATE
RG_WIDTH = 256
RG_HEADS = 4
RG_HEAD_DIM = 64
RG_CONV = 4
RG_C = 8.0
D_FF = 2816
FFN_CONV = 3

LANES = 128
SUBLANES = 8
VMEM_LIMIT = 56 * 1024 * 1024

OFF_Q = 0
OFF_K = 256
OFF_V = 512
OFF_G = 1024
OFF_S5 = 1536
OFF_RGX = 1792
OFF_RGG = 2048
OFF_GKL = 2304
D_PROJ = 2432

TC = GLA_CHUNK
PITCH = TC + SUBLANES
ROWS = BATCH * TC
TM = 1024
FF_CHUNK = 256
PROJ_CHUNK = 256
P0, P1, P2, P3, P4, P5 = 1, 1, 1, 1, 1, 1
SCAN_PROJ_AT = (15, 37, 58)

V_BGK, V_S5D, V_BGLU, V_S5N, V_CW0, V_CB, V_BA, V_BX, V_LAM, V_RGN = 0, 1, 2, 3, 4, 8, 9, 10, 11, 12

f32 = jnp.float32
bf16 = jnp.bfloat16


def _rms(x, g):
    return x * lax.rsqrt(jnp.mean(x * x, axis=-1, keepdims=True) + EPS) * g


def _softplus(x):
    return jnp.maximum(x, 0.0) + jnp.log1p(jnp.exp(-jnp.abs(x)))


def _dot(a, b):
    return jnp.dot(a, b, preferred_element_type=f32)


def _dot_nt(a, b):
    return lax.dot_general(a, b, (((1,), (1,)), ((), ())), preferred_element_type=f32)


def _dot_tn(a, b):
    return lax.dot_general(a, b, (((0,), (0,)), ((), ())), preferred_element_type=f32)


def _to_time_major(ref, val):
    n = ref.shape[0]
    for j in range(n):
        for b in range(BATCH):
            ref[j, b * PITCH:b * PITCH + TC, :] = val[b * TC:(b + 1) * TC, j * LANES:(j + 1) * LANES]
    return jnp.concatenate(
        [jnp.concatenate([ref[j, pl.ds(t, BATCH, stride=PITCH), :] for j in range(n)], axis=-1)
         for t in range(TC)], axis=0)


def _to_batch_major(ref, val):
    n = ref.shape[0]
    for j in range(n):
        for t in range(TC):
            ref[j, pl.ds(t, BATCH, stride=PITCH), :] = val[t * BATCH:(t + 1) * BATCH, j * LANES:(j + 1) * LANES]
    return jnp.concatenate(
        [jnp.concatenate([ref[j, b * PITCH:b * PITCH + TC, :] for j in range(n)], axis=-1)
         for b in range(BATCH)], axis=0)


def _split(x):
    hi = x.astype(bf16)
    return hi, (x - hi.astype(f32)).astype(bf16)


def _mixer_kernel(x_ref, xn_ref, ng_ref, win_ref, wgk_ref, vec_ref, gn_ref, lam_ref, bd_ref, cd_ref,
                  wglu_ref, wa_ref, wx_ref, wout_ref, o_ref,
                  pa_ref, pb_ref, st_ref, s5_ref, rgh_ref, rgprev_ref, bu_ref, a_ref, xin_ref, y_ref,
                  pu_ref, px_ref, pg_ref, pys_ref, pyr_ref, hn_ref):
    step = pl.program_id(0)

    @pl.when(step == 0)
    def _():
        st_ref[...] = jnp.zeros_like(st_ref)
        s5_ref[...] = jnp.zeros_like(s5_ref)
        rgh_ref[...] = jnp.zeros_like(rgh_ref)
        rgprev_ref[...] = jnp.zeros_like(rgprev_ref)
        hin = _rms(x_ref[...].reshape(ROWS, D_MODEL), ng_ref[...]).astype(bf16)
        pa_ref[...] = _dot(hin, win_ref[...])

    def vec(r):
        return vec_ref[r:r + 1, :]

    def _mixer_tile(p_ref, pn_ref):
        hn_ref[...] = _rms(xn_ref[...].reshape(ROWS, D_MODEL), ng_ref[...]).astype(bf16)
        pending = [(c, min(PROJ_CHUNK, D_PROJ - c)) for c in range(0, D_PROJ, PROJ_CHUNK)]

        def next_proj(count):
            for _ in range(min(count, len(pending))):
                c, w = pending.pop(0)
                pn_ref[:, c:c + w] = _dot(hn_ref[...], win_ref[:, c:c + w])

        def seg(off, width):
            return p_ref[:, off:off + width]

        def seg3(off, width):
            return p_ref[:, off:off + width].reshape(BATCH, TC, width)

        u = _to_time_major(pu_ref, seg(OFF_S5, S5_WIDTH))
        next_proj(P0)
        bu_ref[...] = _dot(u.astype(bf16), bd_ref[...])
        next_proj(P1)

        rx = _to_time_major(px_ref, seg(OFF_RGX, RG_WIDTH))
        prev = rgprev_ref[...]
        xc = rx * vec(V_CW0 + RG_CONV - 1) + vec(V_CB)
        for s in range(1, RG_CONV):
            shifted = jnp.concatenate([prev[(RG_CONV - 1 - s) * BATCH:, :], rx[:ROWS - s * BATCH, :]], axis=0)
            xc = xc + shifted * vec(V_CW0 + RG_CONV - 1 - s)
        rgprev_ref[...] = rx[ROWS - (RG_CONV - 1) * BATCH:, :]
        xcb = xc.astype(bf16)
        r = jax.nn.sigmoid(_dot(xcb, wa_ref[...]) + vec(V_BA))
        i = jax.nn.sigmoid(_dot(xcb, wx_ref[...]) + vec(V_BX))
        log_a = -RG_C * r * _softplus(-vec(V_LAM))
        a = jnp.exp(log_a)
        a_ref[...] = a
        xin_ref[...] = jnp.sqrt(-jnp.tanh(log_a) * (a * a + 1.0)) * (i * xc)
        next_proj(P2)

        n_s5 = S5_FLAT // LANES
        n_rg = RG_WIDTH // LANES
        lre = [jnp.broadcast_to(lam_ref[0:1, j * LANES:(j + 1) * LANES], (BATCH, LANES)) for j in range(n_s5)]
        lim = [jnp.broadcast_to(lam_ref[1:2, j * LANES:(j + 1) * LANES], (BATCH, LANES)) for j in range(n_s5)]

        def scan_step(t, carry):
            sre, sim, h = carry
            rows = slice(t * BATCH, (t + 1) * BATCH)
            nre, nim, nh = [], [], []
            for j in range(n_s5):
                c_re = slice(j * LANES, (j + 1) * LANES)
                c_im = slice(S5_FLAT + j * LANES, S5_FLAT + (j + 1) * LANES)
                re = lre[j] * sre[j] - lim[j] * sim[j] + bu_ref[rows, c_re]
                im = lre[j] * sim[j] + lim[j] * sre[j] + bu_ref[rows, c_im]
                bu_ref[rows, c_re] = re
                bu_ref[rows, c_im] = im
                nre.append(re)
                nim.append(im)
            for j in range(n_rg):
                cj = slice(j * LANES, (j + 1) * LANES)
                hj = a_ref[rows, cj] * h[j] + xin_ref[rows, cj]
                xin_ref[rows, cj] = hj
                nh.append(hj)
            return tuple(nre), tuple(nim), tuple(nh)

        carry = (tuple(s5_ref[j] for j in range(n_s5)),
                 tuple(s5_ref[n_s5 + j] for j in range(n_s5)),
                 tuple(rgh_ref[j] for j in range(n_rg)))
        for t in range(TC):
            carry = scan_step(t, carry)
            if t in SCAN_PROJ_AT:
                next_proj(1)
        sre, sim, h = carry
        for j in range(n_s5):
            s5_ref[j] = sre[j]
            s5_ref[n_s5 + j] = sim[j]
        for j in range(n_rg):
            rgh_ref[j] = h[j]

        ys = _dot(bu_ref[...].astype(bf16), cd_ref[...]) + vec(V_S5D) * u
        ys = jax.nn.gelu(ys)
        next_proj(P3)
        ys = ys * jax.nn.sigmoid(_dot(ys.astype(bf16), wglu_ref[...]) + vec(V_BGLU))
        y_ref[:, GLA_DV_TOTAL:GLA_DV_TOTAL + S5_WIDTH] = _to_batch_major(pys_ref, _rms(ys, vec(V_S5N))).astype(bf16)
        next_proj(P4)

        yr = xin_ref[...] * jax.nn.gelu(_to_time_major(pg_ref, seg(OFF_RGG, RG_WIDTH)))
        y_ref[:, GLA_DV_TOTAL + S5_WIDTH:] = _to_batch_major(pyr_ref, _rms(yr, vec(V_RGN))).astype(bf16)
        next_proj(P5)

        gl_hi, gl_lo = _split(seg(OFF_GKL, LANES))
        wg_hi, wg_lo = _split(wgk_ref[...])
        gk_pre = _dot(gl_hi, wg_hi) + _dot(gl_lo, wg_hi) + _dot(gl_hi, wg_lo) + vec(V_BGK)
        gk = (jnp.minimum(gk_pre, 0.0) - jnp.log1p(jnp.exp(-jnp.abs(gk_pre)))) / GLA_GATE_NORM
        ri = lax.broadcasted_iota(jnp.int32, (ROWS, ROWS), 0)
        ci = lax.broadcasted_iota(jnp.int32, (ROWS, ROWS), 1)
        tril = jnp.where((ri - ci >= 0) & (ri - ci <= (ri & (TC - 1))), 1.0, 0.0).astype(bf16)
        gk_hi, gk_lo = _split(gk)
        bc = _dot(tril, gk_hi) + _dot(tril, gk_lo)
        bc3 = bc.reshape(BATCH, TC, GLA_DK_TOTAL)
        bmid = bc3[:, TC // 2 - 1:TC // 2, :]
        blast = bc3[:, TC - 1:TC, :]
        q3 = seg3(OFF_Q, GLA_DK_TOTAL) * (GLA_DK ** -0.5)
        k3 = seg3(OFF_K, GLA_DK_TOTAL)
        qs3 = q3 * jnp.exp(bc3 - bmid)
        kt3 = (k3 * jnp.exp(bmid - bc3)).astype(bf16)
        qe3 = (q3 * jnp.exp(bc3)).astype(bf16)
        kl3 = (k3 * jnp.exp(blast - bc3)).astype(bf16)
        decay3 = jnp.exp(blast)
        next_proj(len(pending))

        lane_head = lax.broadcasted_iota(jnp.int32, (TC, GLA_DK_TOTAL), 1) >> 6
        causal = (lax.broadcasted_iota(jnp.int32, (TC, TC), 1)
                  <= lax.broadcasted_iota(jnp.int32, (TC, TC), 0))
        st_mask = (lax.broadcasted_iota(jnp.int32, (GLA_DV_TOTAL, GLA_DK_TOTAL), 0) >> 7
                   == lax.broadcasted_iota(jnp.int32, (GLA_DV_TOTAL, GLA_DK_TOTAL), 1) >> 6)
        gn = gn_ref[...]

        for b in range(BATCH):
            vb = p_ref[b * TC:(b + 1) * TC, OFF_V:OFF_V + GLA_DV_TOTAL].astype(bf16)
            qs = qs3[b]
            qstack = jnp.concatenate(
                [jnp.where(lane_head == hd, qs, 0.0) for hd in range(GLA_HEADS)], axis=0).astype(bf16)
            scores = _dot_nt(qstack, kt3[b])
            st_b = st_ref[b]
            o_inter = _dot_nt(qe3[b], st_b.astype(bf16))
            gb = p_ref[b * TC:(b + 1) * TC, OFF_G:OFF_G + GLA_DV_TOTAL]
            for hd in range(GLA_HEADS):
                sc = jnp.where(causal, scores[hd * TC:(hd + 1) * TC, :], 0.0).astype(bf16)
                cols = slice(hd * GLA_DV, (hd + 1) * GLA_DV)
                o = _dot(sc, vb[:, cols]) + o_inter[:, cols]
                g_h = gb[:, cols]
                y_ref[b * TC:(b + 1) * TC, cols] = (_rms(o, gn) * jax.nn.silu(g_h)).astype(bf16)
            kv_t = _dot_tn(vb, kl3[b])
            st_ref[b] = st_b * decay3[b] + jnp.where(st_mask, kv_t, 0.0)

        out = _dot(y_ref[...], wout_ref[...])
        o_ref[...] = x_ref[...] + out.reshape(BATCH, TC, D_MODEL)

    @pl.when((step & 1) == 0)
    def _():
        _mixer_tile(pa_ref, pb_ref)

    @pl.when((step & 1) == 1)
    def _():
        _mixer_tile(pb_ref, pa_ref)


def _mixer(x3, attn_g, prm, layer):
    n_t = SEQ // TC

    def lay(shape):
        return pl.BlockSpec((None,) + shape, lambda i: (layer,) + (0,) * len(shape),
                            pipeline_mode=pl.Buffered(1))

    return pl.pallas_call(
        _mixer_kernel,
        grid=(n_t,),
        in_specs=[
            pl.BlockSpec((BATCH, TC, D_MODEL), lambda i: (0, i, 0)),
            pl.BlockSpec((BATCH, TC, D_MODEL), lambda i: (0, jnp.minimum(i + 1, n_t - 1), 0)),
            lay((1, D_MODEL)),
            lay((D_MODEL, D_PROJ)),
            lay((LANES, GLA_DK_TOTAL)),
            lay((16, 256)),
            lay((1, GLA_DV)),
            lay((2, S5_FLAT)),
            lay((S5_WIDTH, 2 * S5_FLAT)),
            lay((2 * S5_FLAT, S5_WIDTH)),
            lay((S5_WIDTH, S5_WIDTH)),
            lay((RG_WIDTH, RG_WIDTH)),
            lay((RG_WIDTH, RG_WIDTH)),
            lay((D_MODEL, D_MODEL)),
        ],
        out_specs=pl.BlockSpec((BATCH, TC, D_MODEL), lambda i: (0, i, 0)),
        out_shape=jax.ShapeDtypeStruct((BATCH, SEQ, D_MODEL), f32),
        scratch_shapes=[
            pltpu.VMEM((ROWS, D_PROJ), f32),
            pltpu.VMEM((ROWS, D_PROJ), f32),
            pltpu.VMEM((BATCH, GLA_DV_TOTAL, GLA_DK_TOTAL), f32),
            pltpu.VMEM((2 * S5_FLAT // LANES, BATCH, LANES), f32),
            pltpu.VMEM((RG_WIDTH // LANES, BATCH, LANES), f32),
            pltpu.VMEM(((RG_CONV - 1) * BATCH, RG_WIDTH), f32),
            pltpu.VMEM((ROWS, 2 * S5_FLAT), f32),
            pltpu.VMEM((ROWS, RG_WIDTH), f32),
            pltpu.VMEM((ROWS, RG_WIDTH), f32),
            pltpu.VMEM((ROWS, D_MODEL), bf16),
        ] + [pltpu.VMEM((S5_WIDTH // LANES, BATCH * PITCH, LANES), f32)] * 5 + [
            pltpu.VMEM((ROWS, D_MODEL), bf16)],
        compiler_params=pltpu.CompilerParams(
            dimension_semantics=("arbitrary",), vmem_limit_bytes=VMEM_LIMIT),
        name="mixer",
    )(x3, x3, attn_g, prm["w_in"], prm["wgk"], prm["vec"], prm["gla_norm"], prm["lam_bar"], prm["bd"], prm["cd"],
      prm["w_glu"], prm["wa"], prm["wx"], prm["w_out"])


def _ffn_kernel(x_ref, g_ref, wup_ref, cw_ref, cb_ref, wdn_ref, fg_ref, o_ref, carry_ref, act_ref, *, last):
    @pl.when(pl.program_id(1) == 0)
    def _():
        carry_ref[...] = jnp.zeros_like(carry_ref)

    x = x_ref[...]
    h = _rms(x, g_ref[...]).astype(bf16)
    row = lax.broadcasted_iota(jnp.int32, (TM, FF_CHUNK), 0)
    for j in range(D_FF // FF_CHUNK):
        cols = slice(j * FF_CHUNK, (j + 1) * FF_CHUNK)
        up = _dot(h, wup_ref[:, cols])
        gv = _dot(h, wup_ref[:, D_FF + j * FF_CHUNK:D_FF + (j + 1) * FF_CHUNK])
        conv = up * cw_ref[FFN_CONV - 1:FFN_CONV, cols] + cb_ref[:, cols]
        for s in range(1, FFN_CONV):
            shifted = pltpu.roll(up, s, 0)
            for t in range(s):
                shifted = jnp.where(row == t, carry_ref[SUBLANES - s + t:SUBLANES - s + t + 1, cols], shifted)
            conv = conv + shifted * cw_ref[FFN_CONV - 1 - s:FFN_CONV - s, cols]
        carry_ref[:, cols] = up[TM - SUBLANES:, :]
        act_ref[:, cols] = (jax.nn.gelu(conv) * gv).astype(bf16)
    out = x + _dot(act_ref[...], wdn_ref[...])
    o_ref[...] = _rms(out, fg_ref[...]) if last else out


def _ffn(x3, mlp_norm, w_up, conv_w, conv_b, w_down, final_g, layer):
    def lay(shape):
        return pl.BlockSpec((None,) + shape, lambda b, i: (layer,) + (0,) * len(shape),
                            pipeline_mode=pl.Buffered(1))

    return pl.pallas_call(
        functools.partial(_ffn_kernel, last=(layer == DEPTH - 1)),
        grid=(BATCH, SEQ // TM),
        in_specs=[
            pl.BlockSpec((None, TM, D_MODEL), lambda b, i: (b, i, 0)),
            lay((1, D_MODEL)),
            lay((D_MODEL, 2 * D_FF)),
            lay((FFN_CONV, D_FF)),
            lay((1, D_FF)),
            lay((D_FF, D_MODEL)),
            pl.BlockSpec((1, D_MODEL), lambda b, i: (0, 0)),
        ],
        out_specs=pl.BlockSpec((None, TM, D_MODEL), lambda b, i: (b, i, 0)),
        out_shape=jax.ShapeDtypeStruct((BATCH, SEQ, D_MODEL), f32),
        scratch_shapes=[pltpu.VMEM((SUBLANES, D_FF), f32),
                        pltpu.VMEM((TM, D_FF), bf16)],
        compiler_params=pltpu.CompilerParams(
            dimension_semantics=("arbitrary", "arbitrary"), vmem_limit_bytes=VMEM_LIMIT),
        name="ffn",
    )(x3, mlp_norm, w_up, conv_w, conv_b, w_down, final_g)


def _block_diag(w):
    nl, nh, n, m = w.shape
    eye = jnp.eye(nh, dtype=w.dtype)
    return jnp.einsum('lhnm,hk->lhnkm', w, eye).reshape(nl, nh * n, nh * m)


def _prepare(w_in, gla_w_gk_up, gla_b_gk, gla_norm, s5_lambda_re, s5_lambda_im, s5_log_dt,
             s5_b_re, s5_b_im, s5_c_re, s5_c_im, s5_d, s5_w_glu, s5_b_glu, s5_norm,
             rg_conv_w, rg_conv_b, rg_w_a, rg_b_a, rg_w_x, rg_b_x, rg_lambda, rg_norm, w_out):
    nl = w_in.shape[0]
    gkl0 = 2 * GLA_DK_TOTAL + 2 * GLA_DV_TOTAL
    w_in_b = w_in.astype(bf16)
    w_in_r = jnp.concatenate(
        [w_in_b[..., :gkl0], w_in_b[..., gkl0 + GLA_LOW_RANK:], w_in_b[..., gkl0:gkl0 + GLA_LOW_RANK],
         jnp.zeros((nl, D_MODEL, LANES - GLA_LOW_RANK), bf16)], axis=-1)
    wgk = jnp.concatenate(
        [gla_w_gk_up, jnp.zeros((nl, LANES - GLA_LOW_RANK, GLA_DK_TOTAL), f32)], axis=1)

    lam = lax.complex(s5_lambda_re, s5_lambda_im)
    dt = jnp.exp(s5_log_dt)[..., None]
    lam_bar = jnp.exp(lam * dt)
    b_bar = ((lam_bar - 1.0) / lam)[..., None] * lax.complex(s5_b_re, s5_b_im)
    bt = jnp.swapaxes(b_bar, 2, 3)
    bd = jnp.concatenate([_block_diag(bt.real), _block_diag(bt.imag)], axis=-1).astype(bf16)
    ct = jnp.swapaxes(lax.complex(s5_c_re, s5_c_im), 2, 3)
    cd = jnp.concatenate([_block_diag(ct.real), -_block_diag(ct.imag)], axis=1).astype(bf16)
    lam_flat = jnp.stack([lam_bar.real.reshape(nl, S5_FLAT), lam_bar.imag.reshape(nl, S5_FLAT)], axis=1)

    rows = [gla_b_gk, s5_d, s5_b_glu, s5_norm,
            rg_conv_w[:, 0], rg_conv_w[:, 1], rg_conv_w[:, 2], rg_conv_w[:, 3],
            rg_conv_b, rg_b_a, rg_b_x, rg_lambda, rg_norm]
    vec = jnp.stack(rows + [jnp.zeros_like(s5_d)] * (16 - len(rows)), axis=1)
    return dict(
        w_in=w_in_r, wgk=wgk, vec=vec, gla_norm=gla_norm[:, None, :], lam_bar=lam_flat,
        bd=bd, cd=cd, w_glu=s5_w_glu.astype(bf16),
        wa=_block_diag(rg_w_a).astype(bf16), wx=_block_diag(rg_w_x).astype(bf16),
        w_out=w_out.astype(bf16))


def kernel(x, attn_norm, w_in, gla_w_gk_up, gla_b_gk, gla_norm, s5_lambda_re, s5_lambda_im, s5_log_dt, s5_b_re, s5_b_im, s5_c_re, s5_c_im, s5_d, s5_w_glu, s5_b_glu, s5_norm, rg_conv_w, rg_conv_b, rg_w_a, rg_b_a, rg_w_x, rg_b_x, rg_lambda, rg_norm, w_out, mlp_norm, w_up, mlp_conv_w, mlp_conv_b, w_down, final_norm):
    prm = _prepare(w_in, gla_w_gk_up, gla_b_gk, gla_norm, s5_lambda_re, s5_lambda_im, s5_log_dt,
                   s5_b_re, s5_b_im, s5_c_re, s5_c_im, s5_d, s5_w_glu, s5_b_glu, s5_norm,
                   rg_conv_w, rg_conv_b, rg_w_a, rg_b_a, rg_w_x, rg_b_x, rg_lambda, rg_norm, w_out)
    attn_g = attn_norm[:, None, :]
    mlp_g = mlp_norm[:, None, :]
    w_up_b = w_up.astype(bf16)
    w_down_b = w_down.astype(bf16)
    conv_b = mlp_conv_b[:, None, :]
    for layer in range(DEPTH):
        x = _mixer(x, attn_g, prm, layer)
        x = _ffn(x, mlp_g, w_up_b, mlp_conv_w, conv_b, w_down_b, final_norm[None, :], layer)
    return x
```

```python
import functools

import jax
import jax.numpy as jnp
from jax import lax
from jax.experimental import pallas as pl
from jax.experimental.pallas import tpu as pltpu

D_MODEL = 1024
BATCH = 8
SEQ = 2048
DEPTH = 4
EPS = 1e-6
GLA_HEADS = 4
GLA_DK_TOTAL = 256
GLA_DV_TOTAL = 512
GLA_DK = 64
GLA_DV = 128
GLA_LOW_RANK = 16
GLA_GATE_NORM = 16.0
GLA_CHUNK = 64
S5_WIDTH = 256
S5_GROUP = 16
S5_GROUPS = 16
S5_STATE = 64
S5_FLAT = S5_GROUPS * S5_STATE
RG_WIDTH = 256
RG_HEADS = 4
RG_HEAD_DIM = 64
RG_CONV = 4
RG_C = 8.0
D_FF = 2816
FFN_CONV = 3

LANES = 128
SUBLANES = 8
VMEM_LIMIT = 56 * 1024 * 1024

D_IN = 2320
OFF_Q = 0
OFF_K = 256
OFF_V = 512
OFF_G = 1024
OFF_GKL = 1536
SRC_S5 = OFF_GKL + GLA_LOW_RANK
OFF_S5 = OFF_GKL + LANES
OFF_RGX = OFF_S5 + S5_WIDTH
OFF_RGG = OFF_RGX + RG_WIDTH
D_PROJ = OFF_RGG + RG_WIDTH

TC = GLA_CHUNK
PITCH = TC + SUBLANES
ROWS = BATCH * TC
TM = 1024
FF_CHUNK = 256
PROJ_CHUNK = 256
P0, P1, P2, P3, P4, P5 = 1, 1, 1, 1, 1, 1
SCAN_PROJ_AT = (15, 37, 58)

V_BGK, V_S5D, V_BGLU, V_S5N, V_CW0, V_CB, V_BA, V_BX, V_LAM, V_RGN = 0, 1, 2, 3, 4, 8, 9, 10, 11, 12

f32 = jnp.float32
bf16 = jnp.bfloat16


def _rms(x, g):
    return x * lax.rsqrt(jnp.mean(x * x, axis=-1, keepdims=True) + EPS) * g


def _softplus(x):
    return jnp.maximum(x, 0.0) + jnp.log1p(jnp.exp(-jnp.abs(x)))


def _dot(a, b):
    return jnp.dot(a, b, preferred_element_type=f32)


def _dot_nt(a, b):
    return lax.dot_general(a, b, (((1,), (1,)), ((), ())), preferred_element_type=f32)


def _dot_tn(a, b):
    return lax.dot_general(a, b, (((0,), (0,)), ((), ())), preferred_element_type=f32)


def _to_time_major(ref, val):
    n = ref.shape[0]
    for j in range(n):
        for b in range(BATCH):
            ref[j, b * PITCH:b * PITCH + TC, :] = val[b * TC:(b + 1) * TC, j * LANES:(j + 1) * LANES]
    return jnp.concatenate(
        [jnp.concatenate([ref[j, pl.ds(t, BATCH, stride=PITCH), :] for j in range(n)], axis=-1)
         for t in range(TC)], axis=0)


def _to_batch_major(ref, val):
    n = ref.shape[0]
    for j in range(n):
        for t in range(TC):
            ref[j, pl.ds(t, BATCH, stride=PITCH), :] = val[t * BATCH:(t + 1) * BATCH, j * LANES:(j + 1) * LANES]
    return jnp.concatenate(
        [jnp.concatenate([ref[j, b * PITCH:b * PITCH + TC, :] for j in range(n)], axis=-1)
         for b in range(BATCH)], axis=0)


def _split(x):
    hi = x.astype(bf16)
    return hi, (x - hi.astype(f32)).astype(bf16)


def _mixer_kernel(x_ref, xn_ref, ng_ref, win_ref, wgk_ref, vec_ref, gn_ref, lam_ref, bd_ref, cd_ref,
                  wglu_ref, wa_ref, wx_ref, wout_ref, o_ref,
                  pa_ref, pb_ref, st_ref, s5_ref, rgh_ref, rgprev_ref, bu_ref, a_ref, xin_ref, y_ref,
                  pu_ref, px_ref, pg_ref, pys_ref, pyr_ref, hn_ref, wsc_ref):
    step = pl.program_id(0)

    @pl.when(step == 0)
    def _():
        wsc_ref[:, 0:OFF_S5] = win_ref[:, 0:OFF_S5]
        wsc_ref[:, OFF_S5:D_PROJ] = win_ref[:, SRC_S5:D_IN]
        st_ref[...] = jnp.zeros_like(st_ref)
        s5_ref[...] = jnp.zeros_like(s5_ref)
        rgh_ref[...] = jnp.zeros_like(rgh_ref)
        rgprev_ref[...] = jnp.zeros_like(rgprev_ref)
        hin = _rms(x_ref[...].reshape(ROWS, D_MODEL), ng_ref[...]).astype(bf16)
        pa_ref[...] = _dot(hin, wsc_ref[...])

    def vec(r):
        return vec_ref[r:r + 1, :]

    def _mixer_tile(p_ref, pn_ref):
        hn_ref[...] = _rms(xn_ref[...].reshape(ROWS, D_MODEL), ng_ref[...]).astype(bf16)
        pending = [(c, min(PROJ_CHUNK, D_PROJ - c)) for c in range(0, D_PROJ, PROJ_CHUNK)]

        def next_proj(count):
            for _ in range(min(count, len(pending))):
                c, w = pending.pop(0)
                pn_ref[:, c:c + w] = _dot(hn_ref[...], wsc_ref[:, c:c + w])

        def seg(off, width):
            return p_ref[:, off:off + width]

        def seg3(off, width):
            return p_ref[:, off:off + width].reshape(BATCH, TC, width)

        u = _to_time_major(pu_ref, seg(OFF_S5, S5_WIDTH))
        next_proj(P0)
        bu_ref[...] = _dot(u.astype(bf16), bd_ref[...])
        next_proj(P1)

        rx = _to_time_major(px_ref, seg(OFF_RGX, RG_WIDTH))
        prev = rgprev_ref[...]
        xc = rx * vec(V_CW0 + RG_CONV - 1) + vec(V_CB)
        for s in range(1, RG_CONV):
            shifted = jnp.concatenate([prev[(RG_CONV - 1 - s) * BATCH:, :], rx[:ROWS - s * BATCH, :]], axis=0)
            xc = xc + shifted * vec(V_CW0 + RG_CONV - 1 - s)
        rgprev_ref[...] = rx[ROWS - (RG_CONV - 1) * BATCH:, :]
        xcb = xc.astype(bf16)
        r = jax.nn.sigmoid(_dot(xcb, wa_ref[...]) + vec(V_BA))
        i = jax.nn.sigmoid(_dot(xcb, wx_ref[...]) + vec(V_BX))
        log_a = -RG_C * r * _softplus(-vec(V_LAM))
        a = jnp.exp(log_a)
        a_ref[...] = a
        xin_ref[...] = jnp.sqrt(-jnp.tanh(log_a) * (a * a + 1.0)) * (i * xc)
        next_proj(P2)

        n_s5 = S5_FLAT // LANES
        n_rg = RG_WIDTH // LANES
        lre = [jnp.broadcast_to(lam_ref[0:1, j * LANES:(j + 1) * LANES], (BATCH, LANES)) for j in range(n_s5)]
        lim = [jnp.broadcast_to(lam_ref[1:2, j * LANES:(j + 1) * LANES], (BATCH, LANES)) for j in range(n_s5)]

        def scan_step(t, carry):
            sre, sim, h = carry
            rows = slice(t * BATCH, (t + 1) * BATCH)
            nre, nim, nh = [], [], []
            for j in range(n_s5):
                c_re = slice(j * LANES, (j + 1) * LANES)
                c_im = slice(S5_FLAT + j * LANES, S5_FLAT + (j + 1) * LANES)
                re = lre[j] * sre[j] - lim[j] * sim[j] + bu_ref[rows, c_re]
                im = lre[j] * sim[j] + lim[j] * sre[j] + bu_ref[rows, c_im]
                bu_ref[rows, c_re] = re
                bu_ref[rows, c_im] = im
                nre.append(re)
                nim.append(im)
            for j in range(n_rg):
                cj = slice(j * LANES, (j + 1) * LANES)
                hj = a_ref[rows, cj] * h[j] + xin_ref[rows, cj]
                xin_ref[rows, cj] = hj
                nh.append(hj)
            return tuple(nre), tuple(nim), tuple(nh)

        carry = (tuple(s5_ref[j] for j in range(n_s5)),
                 tuple(s5_ref[n_s5 + j] for j in range(n_s5)),
                 tuple(rgh_ref[j] for j in range(n_rg)))
        for t in range(TC):
            carry = scan_step(t, carry)
            if t in SCAN_PROJ_AT:
                next_proj(1)
        sre, sim, h = carry
        for j in range(n_s5):
            s5_ref[j] = sre[j]
            s5_ref[n_s5 + j] = sim[j]
        for j in range(n_rg):
            rgh_ref[j] = h[j]

        ys = _dot(bu_ref[...].astype(bf16), cd_ref[...]) + vec(V_S5D) * u
        ys = jax.nn.gelu(ys)
        next_proj(P3)
        ys = ys * jax.nn.sigmoid(_dot(ys.astype(bf16), wglu_ref[...]) + vec(V_BGLU))
        y_ref[:, GLA_DV_TOTAL:GLA_DV_TOTAL + S5_WIDTH] = _to_batch_major(pys_ref, _rms(ys, vec(V_S5N))).astype(bf16)
        next_proj(P4)

        yr = xin_ref[...] * jax.nn.gelu(_to_time_major(pg_ref, seg(OFF_RGG, RG_WIDTH)))
        y_ref[:, GLA_DV_TOTAL + S5_WIDTH:] = _to_batch_major(pyr_ref, _rms(yr, vec(V_RGN))).astype(bf16)
        next_proj(P5)

        gl_hi, gl_lo = _split(seg(OFF_GKL, LANES))
        wg_hi, wg_lo = _split(wgk_ref[...])
        gk_pre = _dot(gl_hi, wg_hi) + _dot(gl_lo, wg_hi) + _dot(gl_hi, wg_lo) + vec(V_BGK)
        gk = (jnp.minimum(gk_pre, 0.0) - jnp.log1p(jnp.exp(-jnp.abs(gk_pre)))) / GLA_GATE_NORM
        ri = lax.broadcasted_iota(jnp.int32, (ROWS, ROWS), 0)
        ci = lax.broadcasted_iota(jnp.int32, (ROWS, ROWS), 1)
        tril = jnp.where((ri - ci >= 0) & (ri - ci <= (ri & (TC - 1))), 1.0, 0.0).astype(bf16)
        gk_hi, gk_lo = _split(gk)
        bc = _dot(tril, gk_hi) + _dot(tril, gk_lo)
        bc3 = bc.reshape(BATCH, TC, GLA_DK_TOTAL)
        bmid = bc3[:, TC // 2 - 1:TC // 2, :]
        blast = bc3[:, TC - 1:TC, :]
        q3 = seg3(OFF_Q, GLA_DK_TOTAL) * (GLA_DK ** -0.5)
        k3 = seg3(OFF_K, GLA_DK_TOTAL)
        qs3 = q3 * jnp.exp(bc3 - bmid)
        kt3 = (k3 * jnp.exp(bmid - bc3)).astype(bf16)
        qe3 = (q3 * jnp.exp(bc3)).astype(bf16)
        kl3 = (k3 * jnp.exp(blast - bc3)).astype(bf16)
        decay3 = jnp.exp(blast)
        next_proj(len(pending))

        lane_head = lax.broadcasted_iota(jnp.int32, (TC, GLA_DK_TOTAL), 1) >> 6
        causal = (lax.broadcasted_iota(jnp.int32, (TC, TC), 1)
                  <= lax.broadcasted_iota(jnp.int32, (TC, TC), 0))
        gn = gn_ref[...]

        for b in range(BATCH):
            vb = p_ref[b * TC:(b + 1) * TC, OFF_V:OFF_V + GLA_DV_TOTAL].astype(bf16)
            qs = qs3[b]
            qstack = jnp.concatenate(
                [jnp.where(lane_head == hd, qs, 0.0) for hd in range(GLA_HEADS)], axis=0).astype(bf16)
            scores = _dot_nt(qstack, kt3[b])
            st_b = st_ref[b]
            o_inter = _dot_nt(qe3[b], st_b.astype(bf16))
            gb = p_ref[b * TC:(b + 1) * TC, OFF_G:OFF_G + GLA_DV_TOTAL]
            for hd in range(GLA_HEADS):
                sc = jnp.where(causal, scores[hd * TC:(hd + 1) * TC, :], 0.0).astype(bf16)
                cols = slice(hd * GLA_DV, (hd + 1) * GLA_DV)
                o = _dot(sc, vb[:, cols]) + o_inter[:, cols]
                g_h = gb[:, cols]
                y_ref[b * TC:(b + 1) * TC, cols] = (_rms(o, gn) * jax.nn.silu(g_h)).astype(bf16)
            klb = kl3[b]
            for hd in range(GLA_HEADS):
                rows_h = slice(hd * GLA_DV, (hd + 1) * GLA_DV)
                kv_h = _dot_tn(vb[:, rows_h], jnp.where(lane_head == hd, klb, jnp.zeros_like(klb)))
                st_ref[b, rows_h, :] = st_b[rows_h, :] * decay3[b] + kv_h

        out = _dot(y_ref[...], wout_ref[...])
        o_ref[...] = x_ref[...] + out.reshape(BATCH, TC, D_MODEL)

    @pl.when((step & 1) == 0)
    def _():
        _mixer_tile(pa_ref, pb_ref)

    @pl.when((step & 1) == 1)
    def _():
        _mixer_tile(pb_ref, pa_ref)


def _mixer(x3, attn_g, prm, layer):
    n_t = SEQ // TC

    def lay(shape):
        return pl.BlockSpec((None,) + shape, lambda i: (layer,) + (0,) * len(shape),
                            pipeline_mode=pl.Buffered(1))

    return pl.pallas_call(
        _mixer_kernel,
        grid=(n_t,),
        in_specs=[
            pl.BlockSpec((BATCH, TC, D_MODEL), lambda i: (0, i, 0)),
            pl.BlockSpec((BATCH, TC, D_MODEL), lambda i: (0, jnp.minimum(i + 1, n_t - 1), 0)),
            lay((1, D_MODEL)),
            lay((D_MODEL, D_IN)),
            lay((LANES, GLA_DK_TOTAL)),
            lay((16, 256)),
            lay((1, GLA_DV)),
            lay((2, S5_FLAT)),
            lay((S5_WIDTH, 2 * S5_FLAT)),
            lay((2 * S5_FLAT, S5_WIDTH)),
            lay((S5_WIDTH, S5_WIDTH)),
            lay((RG_WIDTH, RG_WIDTH)),
            lay((RG_WIDTH, RG_WIDTH)),
            lay((D_MODEL, D_MODEL)),
        ],
        out_specs=pl.BlockSpec((BATCH, TC, D_MODEL), lambda i: (0, i, 0)),
        out_shape=jax.ShapeDtypeStruct((BATCH, SEQ, D_MODEL), f32),
        scratch_shapes=[
            pltpu.VMEM((ROWS, D_PROJ), f32),
            pltpu.VMEM((ROWS, D_PROJ), f32),
            pltpu.VMEM((BATCH, GLA_DV_TOTAL, GLA_DK_TOTAL), f32),
            pltpu.VMEM((2 * S5_FLAT // LANES, BATCH, LANES), f32),
            pltpu.VMEM((RG_WIDTH // LANES, BATCH, LANES), f32),
            pltpu.VMEM(((RG_CONV - 1) * BATCH, RG_WIDTH), f32),
            pltpu.VMEM((ROWS, 2 * S5_FLAT), f32),
            pltpu.VMEM((ROWS, RG_WIDTH), f32),
            pltpu.VMEM((ROWS, RG_WIDTH), f32),
            pltpu.VMEM((ROWS, D_MODEL), bf16),
        ] + [pltpu.VMEM((S5_WIDTH // LANES, BATCH * PITCH, LANES), f32)] * 5 + [
            pltpu.VMEM((ROWS, D_MODEL), bf16),
            pltpu.VMEM((D_MODEL, D_PROJ), bf16)],
        compiler_params=pltpu.CompilerParams(
            dimension_semantics=("arbitrary",), vmem_limit_bytes=VMEM_LIMIT),
        name="mixer",
    )(x3, x3, attn_g, prm["w_in"], prm["wgk"], prm["vec"], prm["gla_norm"], prm["lam_bar"], prm["bd"], prm["cd"],
      prm["w_glu"], prm["wa"], prm["wx"], prm["w_out"])


def _ffn_kernel(x_ref, g_ref, wup_ref, cw_ref, cb_ref, wdn_ref, fg_ref, o_ref, carry_ref, act_ref, *, last):
    @pl.when(pl.program_id(1) == 0)
    def _():
        carry_ref[...] = jnp.zeros_like(carry_ref)

    x = x_ref[...]
    h = _rms(x, g_ref[...]).astype(bf16)
    row = lax.broadcasted_iota(jnp.int32, (TM, FF_CHUNK), 0)
    for j in range(D_FF // FF_CHUNK):
        cols = slice(j * FF_CHUNK, (j + 1) * FF_CHUNK)
        up = _dot(h, wup_ref[:, cols])
        gv = _dot(h, wup_ref[:, D_FF + j * FF_CHUNK:D_FF + (j + 1) * FF_CHUNK])
        conv = up * cw_ref[FFN_CONV - 1:FFN_CONV, cols] + cb_ref[:, cols]
        for s in range(1, FFN_CONV):
            shifted = pltpu.roll(up, s, 0)
            for t in range(s):
                shifted = jnp.where(row == t, carry_ref[SUBLANES - s + t:SUBLANES - s + t + 1, cols], shifted)
            conv = conv + shifted * cw_ref[FFN_CONV - 1 - s:FFN_CONV - s, cols]
        carry_ref[:, cols] = up[TM - SUBLANES:, :]
        act_ref[:, cols] = (jax.nn.gelu(conv) * gv).astype(bf16)
    out = x + _dot(act_ref[...], wdn_ref[...])
    o_ref[...] = _rms(out, fg_ref[...]) if last else out


def _ffn(x3, mlp_norm, w_up, conv_w, conv_b, w_down, final_g, layer):
    def lay(shape):
        return pl.BlockSpec((None,) + shape, lambda b, i: (layer,) + (0,) * len(shape),
                            pipeline_mode=pl.Buffered(1))

    return pl.pallas_call(
        functools.partial(_ffn_kernel, last=(layer == DEPTH - 1)),
        grid=(BATCH, SEQ // TM),
        in_specs=[
            pl.BlockSpec((None, TM, D_MODEL), lambda b, i: (b, i, 0)),
            lay((1, D_MODEL)),
            lay((D_MODEL, 2 * D_FF)),
            lay((FFN_CONV, D_FF)),
            lay((1, D_FF)),
            lay((D_FF, D_MODEL)),
            pl.BlockSpec((1, D_MODEL), lambda b, i: (0, 0)),
        ],
        out_specs=pl.BlockSpec((None, TM, D_MODEL), lambda b, i: (b, i, 0)),
        out_shape=jax.ShapeDtypeStruct((BATCH, SEQ, D_MODEL), f32),
        scratch_shapes=[pltpu.VMEM((SUBLANES, D_FF), f32),
                        pltpu.VMEM((TM, D_FF), bf16)],
        compiler_params=pltpu.CompilerParams(
            dimension_semantics=("arbitrary", "arbitrary"), vmem_limit_bytes=VMEM_LIMIT),
        name="ffn",
    )(x3, mlp_norm, w_up, conv_w, conv_b, w_down, final_g)


def _block_diag(w):
    nl, nh, n, m = w.shape
    eye = jnp.eye(nh, dtype=w.dtype)
    return jnp.einsum('lhnm,hk->lhnkm', w, eye).reshape(nl, nh * n, nh * m)


def _prepare(w_in, gla_w_gk_up, gla_b_gk, gla_norm, s5_lambda_re, s5_lambda_im, s5_log_dt,
             s5_b_re, s5_b_im, s5_c_re, s5_c_im, s5_d, s5_w_glu, s5_b_glu, s5_norm,
             rg_conv_w, rg_conv_b, rg_w_a, rg_b_a, rg_w_x, rg_b_x, rg_lambda, rg_norm, w_out):
    nl = w_in.shape[0]
    wgk = jnp.concatenate(
        [gla_w_gk_up, jnp.zeros((nl, LANES - GLA_LOW_RANK, GLA_DK_TOTAL), f32)], axis=1)

    lam = lax.complex(s5_lambda_re, s5_lambda_im)
    dt = jnp.exp(s5_log_dt)[..., None]
    lam_bar = jnp.exp(lam * dt)
    b_bar = ((lam_bar - 1.0) / lam)[..., None] * lax.complex(s5_b_re, s5_b_im)
    bt = jnp.swapaxes(b_bar, 2, 3)
    bd = jnp.concatenate([_block_diag(bt.real), _block_diag(bt.imag)], axis=-1).astype(bf16)
    ct = jnp.swapaxes(lax.complex(s5_c_re, s5_c_im), 2, 3)
    cd = jnp.concatenate([_block_diag(ct.real), -_block_diag(ct.imag)], axis=1).astype(bf16)
    lam_flat = jnp.stack([lam_bar.real.reshape(nl, S5_FLAT), lam_bar.imag.reshape(nl, S5_FLAT)], axis=1)

    rows = [gla_b_gk, s5_d, s5_b_glu, s5_norm,
            rg_conv_w[:, 0], rg_conv_w[:, 1], rg_conv_w[:, 2], rg_conv_w[:, 3],
            rg_conv_b, rg_b_a, rg_b_x, rg_lambda, rg_norm]
    vec = jnp.stack(rows + [jnp.zeros_like(s5_d)] * (16 - len(rows)), axis=1)
    return dict(
        w_in=w_in.astype(bf16), wgk=wgk, vec=vec, gla_norm=gla_norm[:, None, :], lam_bar=lam_flat,
        bd=bd, cd=cd, w_glu=s5_w_glu.astype(bf16),
        wa=_block_diag(rg_w_a).astype(bf16), wx=_block_diag(rg_w_x).astype(bf16),
        w_out=w_out.astype(bf16))


def kernel(x, attn_norm, w_in, gla_w_gk_up, gla_b_gk, gla_norm, s5_lambda_re, s5_lambda_im, s5_log_dt, s5_b_re, s5_b_im, s5_c_re, s5_c_im, s5_d, s5_w_glu, s5_b_glu, s5_norm, rg_conv_w, rg_conv_b, rg_w_a, rg_b_a, rg_w_x, rg_b_x, rg_lambda, rg_norm, w_out, mlp_norm, w_up, mlp_conv_w, mlp_conv_b, w_down, final_norm):
    prm = _prepare(w_in, gla_w_gk_up, gla_b_gk, gla_norm, s5_lambda_re, s5_lambda_im, s5_log_dt,
                   s5_b_re, s5_b_im, s5_c_re, s5_c_im, s5_d, s5_w_glu, s5_b_glu, s5_norm,
                   rg_conv_w, rg_conv_b, rg_w_a, rg_b_a, rg_w_x, rg_b_x, rg_lambda, rg_norm, w_out)
    attn_g = attn_norm[:, None, :]
    mlp_g = mlp_norm[:, None, :]
    w_up_b = w_up.astype(bf16)
    w_down_b = w_down.astype(bf16)
    conv_b = mlp_conv_b[:, None, :]
    for layer in range(DEPTH):
        x = _mixer(x, attn_g, prm, layer)
        x = _ffn(x, mlp_g, w_up_b, mlp_conv_w, conv_b, w_down_b, final_norm[None, :], layer)
    return x
```

```python
import functools

import jax
import jax.numpy as jnp
from jax import lax
from jax.experimental import pallas as pl
from jax.experimental.pallas import tpu as pltpu

D_MODEL = 1024
BATCH = 8
SEQ = 2048
DEPTH = 4
EPS = 1e-6
GLA_HEADS = 4
GLA_DK_TOTAL = 256
GLA_DV_TOTAL = 512
GLA_DK = 64
GLA_DV = 128
GLA_LOW_RANK = 16
GLA_GATE_NORM = 16.0
GLA_CHUNK = 64
S5_WIDTH = 256
S5_GROUP = 16
S5_GROUPS = 16
S5_STATE = 64
S5_FLAT = S5_GROUPS * S5_STATE
RG_WIDTH = 256
RG_HEADS = 4
RG_HEAD_DIM = 64
RG_CONV = 4
RG_C = 8.0
D_FF = 2816
FFN_CONV = 3

LANES = 128
SUBLANES = 8
VMEM_LIMIT = 56 * 1024 * 1024

D_IN = 2320
OFF_Q = 0
OFF_K = 256
OFF_V = 512
OFF_G = 1024
OFF_GKL = 1536
SRC_S5 = OFF_GKL + GLA_LOW_RANK
OFF_S5 = OFF_GKL + LANES
OFF_RGX = OFF_S5 + S5_WIDTH
OFF_RGG = OFF_RGX + RG_WIDTH
D_PROJ = OFF_RGG + RG_WIDTH

TC = GLA_CHUNK
PITCH = TC + SUBLANES
ROWS = BATCH * TC
TM = 1024
FF_CHUNK = 256
PROJ_CHUNK = 256
P0, P1, P2, P3, P4, P5 = 1, 1, 1, 1, 1, 1
SCAN_PROJ_AT = (15, 37, 58)

V_BGK, V_S5D, V_BGLU, V_S5N, V_CW0, V_CB, V_BA, V_BX, V_LAM, V_RGN = 0, 1, 2, 3, 4, 8, 9, 10, 11, 12

f32 = jnp.float32
bf16 = jnp.bfloat16


def _rms(x, g):
    return x * lax.rsqrt(jnp.mean(x * x, axis=-1, keepdims=True) + EPS) * g


def _softplus(x):
    return jnp.maximum(x, 0.0) + jnp.log1p(jnp.exp(-jnp.abs(x)))


def _dot(a, b):
    return jnp.dot(a, b, preferred_element_type=f32)


def _dot_nt(a, b):
    return lax.dot_general(a, b, (((1,), (1,)), ((), ())), preferred_element_type=f32)


def _dot_tn(a, b):
    return lax.dot_general(a, b, (((0,), (0,)), ((), ())), preferred_element_type=f32)


def _to_time_major(ref, val):
    n = ref.shape[0]
    for j in range(n):
        for b in range(BATCH):
            ref[j, b * PITCH:b * PITCH + TC, :] = val[b * TC:(b + 1) * TC, j * LANES:(j + 1) * LANES]
    return jnp.concatenate(
        [jnp.concatenate([ref[j, pl.ds(t, BATCH, stride=PITCH), :] for j in range(n)], axis=-1)
         for t in range(TC)], axis=0)


def _to_batch_major(ref, val):
    n = ref.shape[0]
    for j in range(n):
        for t in range(TC):
            ref[j, pl.ds(t, BATCH, stride=PITCH), :] = val[t * BATCH:(t + 1) * BATCH, j * LANES:(j + 1) * LANES]
    return jnp.concatenate(
        [jnp.concatenate([ref[j, b * PITCH:b * PITCH + TC, :] for j in range(n)], axis=-1)
         for b in range(BATCH)], axis=0)


def _split(x):
    hi = x.astype(bf16)
    return hi, (x - hi.astype(f32)).astype(bf16)


def _mixer_kernel(x_ref, xn_ref, ng_ref, win_ref, wgk_ref, vec_ref, gn_ref, lam_ref, bd_ref, cd_ref,
                  wglu_ref, wa_ref, wx_ref, wout_ref, o_ref,
                  pa_ref, pb_ref, st_ref, s5_ref, rgh_ref, rgprev_ref, bu_ref, a_ref, xin_ref, y_ref,
                  pu_ref, px_ref, pg_ref, pys_ref, pyr_ref, hn_ref, wsc_ref):
    step = pl.program_id(0)

    @pl.when(step == 0)
    def _():
        wsc_ref[:, 0:OFF_S5] = win_ref[:, 0:OFF_S5]
        wsc_ref[:, OFF_S5:D_PROJ] = win_ref[:, SRC_S5:D_IN]
        st_ref[...] = jnp.zeros_like(st_ref)
        s5_ref[...] = jnp.zeros_like(s5_ref)
        rgh_ref[...] = jnp.zeros_like(rgh_ref)
        rgprev_ref[...] = jnp.zeros_like(rgprev_ref)
        hin = _rms(x_ref[...].reshape(ROWS, D_MODEL), ng_ref[...]).astype(bf16)
        pa_ref[...] = _dot(hin, wsc_ref[...])

    def vec(r):
        return vec_ref[r:r + 1, :]

    def _mixer_tile(p_ref, pn_ref):
        hn_ref[...] = _rms(xn_ref[...].reshape(ROWS, D_MODEL), ng_ref[...]).astype(bf16)
        pending = [(c, min(PROJ_CHUNK, D_PROJ - c)) for c in range(0, D_PROJ, PROJ_CHUNK)]

        def next_proj(count):
            for _ in range(min(count, len(pending))):
                c, w = pending.pop(0)
                pn_ref[:, c:c + w] = _dot(hn_ref[...], wsc_ref[:, c:c + w])

        def seg(off, width):
            return p_ref[:, off:off + width]

        def seg3(off, width):
            return p_ref[:, off:off + width].reshape(BATCH, TC, width)

        u = _to_time_major(pu_ref, seg(OFF_S5, S5_WIDTH))
        next_proj(P0)
        bu_ref[...] = _dot(u.astype(bf16), bd_ref[...])
        next_proj(P1)

        rx = _to_time_major(px_ref, seg(OFF_RGX, RG_WIDTH))
        prev = rgprev_ref[...]
        xc = rx * vec(V_CW0 + RG_CONV - 1) + vec(V_CB)
        for s in range(1, RG_CONV):
            shifted = jnp.concatenate([prev[(RG_CONV - 1 - s) * BATCH:, :], rx[:ROWS - s * BATCH, :]], axis=0)
            xc = xc + shifted * vec(V_CW0 + RG_CONV - 1 - s)
        rgprev_ref[...] = rx[ROWS - (RG_CONV - 1) * BATCH:, :]
        xcb = xc.astype(bf16)
        r = jax.nn.sigmoid(_dot(xcb, wa_ref[...]) + vec(V_BA))
        i = jax.nn.sigmoid(_dot(xcb, wx_ref[...]) + vec(V_BX))
        log_a = -RG_C * r * _softplus(-vec(V_LAM))
        a = jnp.exp(log_a)
        a_ref[...] = a
        xin_ref[...] = jnp.sqrt(-jnp.tanh(log_a) * (a * a + 1.0)) * (i * xc)
        next_proj(P2)

        n_s5 = S5_FLAT // LANES
        n_rg = RG_WIDTH // LANES
        lre = [jnp.broadcast_to(lam_ref[0:1, j * LANES:(j + 1) * LANES], (BATCH, LANES)) for j in range(n_s5)]
        lim = [jnp.broadcast_to(lam_ref[1:2, j * LANES:(j + 1) * LANES], (BATCH, LANES)) for j in range(n_s5)]

        def scan_step(t, carry):
            sre, sim, h = carry
            rows = slice(t * BATCH, (t + 1) * BATCH)
            nre, nim, nh = [], [], []
            for j in range(n_s5):
                c_re = slice(j * LANES, (j + 1) * LANES)
                c_im = slice(S5_FLAT + j * LANES, S5_FLAT + (j + 1) * LANES)
                re = lre[j] * sre[j] - lim[j] * sim[j] + bu_ref[rows, c_re]
                im = lre[j] * sim[j] + lim[j] * sre[j] + bu_ref[rows, c_im]
                bu_ref[rows, c_re] = re
                bu_ref[rows, c_im] = im
                nre.append(re)
                nim.append(im)
            for j in range(n_rg):
                cj = slice(j * LANES, (j + 1) * LANES)
                hj = a_ref[rows, cj] * h[j] + xin_ref[rows, cj]
                xin_ref[rows, cj] = hj
                nh.append(hj)
            return tuple(nre), tuple(nim), tuple(nh)

        carry = (tuple(s5_ref[j] for j in range(n_s5)),
                 tuple(s5_ref[n_s5 + j] for j in range(n_s5)),
                 tuple(rgh_ref[j] for j in range(n_rg)))
        for t in range(TC):
            carry = scan_step(t, carry)
            if t in SCAN_PROJ_AT:
                next_proj(1)
        sre, sim, h = carry
        for j in range(n_s5):
            s5_ref[j] = sre[j]
            s5_ref[n_s5 + j] = sim[j]
        for j in range(n_rg):
            rgh_ref[j] = h[j]

        ys = _dot(bu_ref[...].astype(bf16), cd_ref[...]) + vec(V_S5D) * u
        ys = jax.nn.gelu(ys)
        next_proj(P3)
        ys = ys * jax.nn.sigmoid(_dot(ys.astype(bf16), wglu_ref[...]) + vec(V_BGLU))
        y_ref[:, GLA_DV_TOTAL:GLA_DV_TOTAL + S5_WIDTH] = _to_batch_major(pys_ref, _rms(ys, vec(V_S5N))).astype(bf16)
        next_proj(P4)

        yr = xin_ref[...] * jax.nn.gelu(_to_time_major(pg_ref, seg(OFF_RGG, RG_WIDTH)))
        y_ref[:, GLA_DV_TOTAL + S5_WIDTH:] = _to_batch_major(pyr_ref, _rms(yr, vec(V_RGN))).astype(bf16)
        next_proj(P5)

        gl_hi, gl_lo = _split(seg(OFF_GKL, LANES))
        wg_hi, wg_lo = _split(wgk_ref[...])
        gk_pre = _dot(gl_hi, wg_hi) + _dot(gl_lo, wg_hi) + _dot(gl_hi, wg_lo) + vec(V_BGK)
        gk = (jnp.minimum(gk_pre, 0.0) - jnp.log1p(jnp.exp(-jnp.abs(gk_pre)))) / GLA_GATE_NORM
        ri = lax.broadcasted_iota(jnp.int32, (ROWS, ROWS), 0)
        ci = lax.broadcasted_iota(jnp.int32, (ROWS, ROWS), 1)
        tril = jnp.where((ri - ci >= 0) & (ri - ci <= (ri & (TC - 1))), 1.0, 0.0).astype(bf16)
        gk_hi, gk_lo = _split(gk)
        bc = _dot(tril, gk_hi) + _dot(tril, gk_lo)
        bc3 = bc.reshape(BATCH, TC, GLA_DK_TOTAL)
        bmid = bc3[:, TC // 2 - 1:TC // 2, :]
        blast = bc3[:, TC - 1:TC, :]
        q3 = seg3(OFF_Q, GLA_DK_TOTAL) * (GLA_DK ** -0.5)
        k3 = seg3(OFF_K, GLA_DK_TOTAL)
        qs3 = q3 * jnp.exp(bc3 - bmid)
        kt3 = (k3 * jnp.exp(bmid - bc3)).astype(bf16)
        qe3 = (q3 * jnp.exp(bc3)).astype(bf16)
        kl3 = (k3 * jnp.exp(blast - bc3)).astype(bf16)
        decay3 = jnp.exp(blast)
        next_proj(len(pending))

        lane_head = lax.broadcasted_iota(jnp.int32, (TC, GLA_DK_TOTAL), 1) >> 6
        causal = (lax.broadcasted_iota(jnp.int32, (TC, TC), 1)
                  <= lax.broadcasted_iota(jnp.int32, (TC, TC), 0))
        gn = gn_ref[...]

        for b in range(BATCH):
            vb = p_ref[b * TC:(b + 1) * TC, OFF_V:OFF_V + GLA_DV_TOTAL].astype(bf16)
            qs = qs3[b]
            qstack = jnp.concatenate(
                [jnp.where(lane_head == hd, qs, 0.0) for hd in range(GLA_HEADS)], axis=0).astype(bf16)
            scores = _dot_nt(qstack, kt3[b])
            st_b = st_ref[b]
            o_inter = _dot_nt(qe3[b], st_b.astype(bf16))
            gb = p_ref[b * TC:(b + 1) * TC, OFF_G:OFF_G + GLA_DV_TOTAL]
            for hd in range(GLA_HEADS):
                sc = jnp.where(causal, scores[hd * TC:(hd + 1) * TC, :], 0.0).astype(bf16)
                cols = slice(hd * GLA_DV, (hd + 1) * GLA_DV)
                o = _dot(sc, vb[:, cols]) + o_inter[:, cols]
                g_h = gb[:, cols]
                y_ref[b * TC:(b + 1) * TC, cols] = (_rms(o, gn) * jax.nn.silu(g_h)).astype(bf16)
            klb = kl3[b]
            for hd in range(GLA_HEADS):
                rows_h = slice(hd * GLA_DV, (hd + 1) * GLA_DV)
                kv_h = _dot_tn(vb[:, rows_h], jnp.where(lane_head == hd, klb, jnp.zeros_like(klb)))
                st_ref[b, rows_h, :] = st_b[rows_h, :] * decay3[b] + kv_h

        out = _dot(y_ref[...], wout_ref[...])
        o_ref[...] = x_ref[...] + out.reshape(BATCH, TC, D_MODEL)

    @pl.when((step & 1) == 0)
    def _():
        _mixer_tile(pa_ref, pb_ref)

    @pl.when((step & 1) == 1)
    def _():
        _mixer_tile(pb_ref, pa_ref)


def _mixer(x3, attn_g, prm, layer):
    n_t = SEQ // TC

    def lay(shape):
        return pl.BlockSpec((None,) + shape, lambda i: (layer,) + (0,) * len(shape),
                            pipeline_mode=pl.Buffered(1))

    return pl.pallas_call(
        _mixer_kernel,
        grid=(n_t,),
        in_specs=[
            pl.BlockSpec((BATCH, TC, D_MODEL), lambda i: (0, i, 0)),
            pl.BlockSpec((BATCH, TC, D_MODEL), lambda i: (0, jnp.minimum(i + 1, n_t - 1), 0)),
            lay((1, D_MODEL)),
            lay((D_MODEL, D_PROJ)),
            lay((LANES, GLA_DK_TOTAL)),
            lay((16, 256)),
            lay((1, GLA_DV)),
            lay((2, S5_FLAT)),
            lay((S5_WIDTH, 2 * S5_FLAT)),
            lay((2 * S5_FLAT, S5_WIDTH)),
            lay((S5_WIDTH, S5_WIDTH)),
            lay((RG_WIDTH, RG_WIDTH)),
            lay((RG_WIDTH, RG_WIDTH)),
            lay((D_MODEL, D_MODEL)),
        ],
        out_specs=pl.BlockSpec((BATCH, TC, D_MODEL), lambda i: (0, i, 0)),
        out_shape=jax.ShapeDtypeStruct((BATCH, SEQ, D_MODEL), f32),
        scratch_shapes=[
            pltpu.VMEM((ROWS, D_PROJ), f32),
            pltpu.VMEM((ROWS, D_PROJ), f32),
            pltpu.VMEM((BATCH, GLA_DV_TOTAL, GLA_DK_TOTAL), f32),
            pltpu.VMEM((2 * S5_FLAT // LANES, BATCH, LANES), f32),
            pltpu.VMEM((RG_WIDTH // LANES, BATCH, LANES), f32),
            pltpu.VMEM(((RG_CONV - 1) * BATCH, RG_WIDTH), f32),
            pltpu.VMEM((ROWS, 2 * S5_FLAT), f32),
            pltpu.VMEM((ROWS, RG_WIDTH), f32),
            pltpu.VMEM((ROWS, RG_WIDTH), f32),
            pltpu.VMEM((ROWS, D_MODEL), bf16),
        ] + [pltpu.VMEM((S5_WIDTH // LANES, BATCH * PITCH, LANES), f32)] * 5 + [
            pltpu.VMEM((ROWS, D_MODEL), bf16),
            pltpu.VMEM((D_MODEL, D_PROJ), bf16)],
        compiler_params=pltpu.CompilerParams(
            dimension_semantics=("arbitrary",), vmem_limit_bytes=VMEM_LIMIT),
        name="mixer",
    )(x3, x3, attn_g, prm["w_in"], prm["wgk"], prm["vec"], prm["gla_norm"], prm["lam_bar"], prm["bd"], prm["cd"],
      prm["w_glu"], prm["wa"], prm["wx"], prm["w_out"])


def _ffn_kernel(x_ref, g_ref, wup_ref, cw_ref, cb_ref, wdn_ref, fg_ref, o_ref, carry_ref, act_ref, *, last):
    @pl.when(pl.program_id(1) == 0)
    def _():
        carry_ref[...] = jnp.zeros_like(carry_ref)

    x = x_ref[...]
    h = _rms(x, g_ref[...]).astype(bf16)
    row = lax.broadcasted_iota(jnp.int32, (TM, FF_CHUNK), 0)
    for j in range(D_FF // FF_CHUNK):
        cols = slice(j * FF_CHUNK, (j + 1) * FF_CHUNK)
        up = _dot(h, wup_ref[:, cols])
        gv = _dot(h, wup_ref[:, D_FF + j * FF_CHUNK:D_FF + (j + 1) * FF_CHUNK])
        conv = up * cw_ref[FFN_CONV - 1:FFN_CONV, cols] + cb_ref[:, cols]
        for s in range(1, FFN_CONV):
            shifted = pltpu.roll(up, s, 0)
            for t in range(s):
                shifted = jnp.where(row == t, carry_ref[SUBLANES - s + t:SUBLANES - s + t + 1, cols], shifted)
            conv = conv + shifted * cw_ref[FFN_CONV - 1 - s:FFN_CONV - s, cols]
        carry_ref[:, cols] = up[TM - SUBLANES:, :]
        act_ref[:, cols] = (jax.nn.gelu(conv) * gv).astype(bf16)
    out = x + _dot(act_ref[...], wdn_ref[...])
    o_ref[...] = _rms(out, fg_ref[...]) if last else out


def _ffn(x3, mlp_norm, w_up, conv_w, conv_b, w_down, final_g, layer):
    def lay(shape):
        return pl.BlockSpec((None,) + shape, lambda b, i: (layer,) + (0,) * len(shape),
                            pipeline_mode=pl.Buffered(1))

    return pl.pallas_call(
        functools.partial(_ffn_kernel, last=(layer == DEPTH - 1)),
        grid=(BATCH, SEQ // TM),
        in_specs=[
            pl.BlockSpec((None, TM, D_MODEL), lambda b, i: (b, i, 0)),
            lay((1, D_MODEL)),
            lay((D_MODEL, 2 * D_FF)),
            lay((FFN_CONV, D_FF)),
            lay((1, D_FF)),
            lay((D_FF, D_MODEL)),
            pl.BlockSpec((1, D_MODEL), lambda b, i: (0, 0)),
        ],
        out_specs=pl.BlockSpec((None, TM, D_MODEL), lambda b, i: (b, i, 0)),
        out_shape=jax.ShapeDtypeStruct((BATCH, SEQ, D_MODEL), f32),
        scratch_shapes=[pltpu.VMEM((SUBLANES, D_FF), f32),
                        pltpu.VMEM((TM, D_FF), bf16)],
        compiler_params=pltpu.CompilerParams(
            dimension_semantics=("arbitrary", "arbitrary"), vmem_limit_bytes=VMEM_LIMIT),
        name="ffn",
    )(x3, mlp_norm, w_up, conv_w, conv_b, w_down, final_g)


def _block_diag(w):
    nl, nh, n, m = w.shape
    eye = jnp.eye(nh, dtype=bool)[None, :, None, :, None]
    return jnp.where(eye, w[:, :, :, None, :], 0.0).reshape(nl, nh * n, nh * m)


def _prepare(w_in, gla_w_gk_up, gla_b_gk, gla_norm, s5_lambda_re, s5_lambda_im, s5_log_dt,
             s5_b_re, s5_b_im, s5_c_re, s5_c_im, s5_d, s5_w_glu, s5_b_glu, s5_norm,
             rg_conv_w, rg_conv_b, rg_w_a, rg_b_a, rg_w_x, rg_b_x, rg_lambda, rg_norm, w_out):
    nl = w_in.shape[0]
    wgk = jnp.concatenate(
        [gla_w_gk_up, jnp.zeros((nl, LANES - GLA_LOW_RANK, GLA_DK_TOTAL), f32)], axis=1)

    lam = lax.complex(s5_lambda_re, s5_lambda_im)
    dt = jnp.exp(s5_log_dt)[..., None]
    lam_bar = jnp.exp(lam * dt)
    b_bar = ((lam_bar - 1.0) / lam)[..., None] * lax.complex(s5_b_re, s5_b_im)
    bt = jnp.swapaxes(b_bar, 2, 3)
    bd = jnp.concatenate([_block_diag(bt.real), _block_diag(bt.imag)], axis=-1).astype(bf16)
    ct = jnp.swapaxes(lax.complex(s5_c_re, s5_c_im), 2, 3)
    cd = jnp.concatenate([_block_diag(ct.real), -_block_diag(ct.imag)], axis=1).astype(bf16)
    lam_flat = jnp.stack([lam_bar.real.reshape(nl, S5_FLAT), lam_bar.imag.reshape(nl, S5_FLAT)], axis=1)

    rows = [gla_b_gk, s5_d, s5_b_glu, s5_norm,
            rg_conv_w[:, 0], rg_conv_w[:, 1], rg_conv_w[:, 2], rg_conv_w[:, 3],
            rg_conv_b, rg_b_a, rg_b_x, rg_lambda, rg_norm]
    vec = jnp.stack(rows + [jnp.zeros_like(s5_d)] * (16 - len(rows)), axis=1)
    return dict(
        w_in=jnp.pad(w_in, ((0, 0), (0, 0), (0, D_PROJ - D_IN))).astype(bf16), wgk=wgk, vec=vec, gla_norm=gla_norm[:, None, :], lam_bar=lam_flat,
        bd=bd, cd=cd, w_glu=s5_w_glu.astype(bf16),
        wa=_block_diag(rg_w_a).astype(bf16), wx=_block_diag(rg_w_x).astype(bf16),
        w_out=w_out.astype(bf16))


def kernel(x, attn_norm, w_in, gla_w_gk_up, gla_b_gk, gla_norm, s5_lambda_re, s5_lambda_im, s5_log_dt, s5_b_re, s5_b_im, s5_c_re, s5_c_im, s5_d, s5_w_glu, s5_b_glu, s5_norm, rg_conv_w, rg_conv_b, rg_w_a, rg_b_a, rg_w_x, rg_b_x, rg_lambda, rg_norm, w_out, mlp_norm, w_up, mlp_conv_w, mlp_conv_b, w_down, final_norm):
    prm = _prepare(w_in, gla_w_gk_up, gla_b_gk, gla_norm, s5_lambda_re, s5_lambda_im, s5_log_dt,
                   s5_b_re, s5_b_im, s5_c_re, s5_c_im, s5_d, s5_w_glu, s5_b_glu, s5_norm,
                   rg_conv_w, rg_conv_b, rg_w_a, rg_b_a, rg_w_x, rg_b_x, rg_lambda, rg_norm, w_out)
    attn_g = attn_norm[:, None, :]
    mlp_g = mlp_norm[:, None, :]
    w_up_b = w_up.astype(bf16)
    w_down_b = w_down.astype(bf16)
    conv_b = mlp_conv_b[:, None, :]
    for layer in range(DEPTH):
        x = _mixer(x, attn_g, prm, layer)
        x = _ffn(x, mlp_g, w_up_b, mlp_conv_w, conv_b, w_down_b, final_norm[None, :], layer)
    return x
```

```python
import functools

import jax
import jax.numpy as jnp
from jax import lax
from jax.experimental import pallas as pl
from jax.experimental.pallas import tpu as pltpu

D_MODEL = 1024
BATCH = 8
SEQ = 2048
DEPTH = 4
EPS = 1e-6
GLA_HEADS = 4
GLA_DK_TOTAL = 256
GLA_DV_TOTAL = 512
GLA_DK = 64
GLA_DV = 128
GLA_LOW_RANK = 16
GLA_GATE_NORM = 16.0
GLA_CHUNK = 64
S5_WIDTH = 256
S5_GROUP = 16
S5_GROUPS = 16
S5_STATE = 64
S5_FLAT = S5_GROUPS * S5_STATE
RG_WIDTH = 256
RG_HEADS = 4
RG_HEAD_DIM = 64
RG_CONV = 4
RG_C = 8.0
D_FF = 2816
FFN_CONV = 3

LANES = 128
SUBLANES = 8
VMEM_LIMIT = 56 * 1024 * 1024

D_IN = 2320
OFF_Q = 0
OFF_K = 256
OFF_V = 512
OFF_G = 1024
OFF_GKL = 1536
SRC_S5 = OFF_GKL + GLA_LOW_RANK
OFF_S5 = OFF_GKL + LANES
OFF_RGX = OFF_S5 + S5_WIDTH
OFF_RGG = OFF_RGX + RG_WIDTH
D_PROJ = OFF_RGG + RG_WIDTH

TC = GLA_CHUNK
PITCH = TC + SUBLANES
ROWS = BATCH * TC
TM = 1024
FF_CHUNK = 256
PROJ_CHUNK = 256
P0, P1, P2, P3, P4, P5 = 1, 1, 1, 1, 1, 1
SCAN_PROJ_AT = (15, 37, 58)

DK_SHIFT = GLA_DK.bit_length() - 1

VEC_ROWS = 16
V_BGK, V_S5D, V_BGLU, V_S5N, V_CW0, V_CB, V_BA, V_BX, V_LAM, V_RGN = 0, 1, 2, 3, 4, 8, 9, 10, 11, 12

f32 = jnp.float32
bf16 = jnp.bfloat16


def _rms(x, g):
    return x * lax.rsqrt(jnp.mean(x * x, axis=-1, keepdims=True) + EPS) * g


def _softplus(x):
    return jnp.maximum(x, 0.0) + jnp.log1p(jnp.exp(-jnp.abs(x)))


def _dot(a, b):
    return jnp.dot(a, b, preferred_element_type=f32)


def _dot_nt(a, b):
    return lax.dot_general(a, b, (((1,), (1,)), ((), ())), preferred_element_type=f32)


def _dot_tn(a, b):
    return lax.dot_general(a, b, (((0,), (0,)), ((), ())), preferred_element_type=f32)


def _to_time_major(ref, val):
    n = ref.shape[0]
    for j in range(n):
        for b in range(BATCH):
            ref[j, b * PITCH:b * PITCH + TC, :] = val[b * TC:(b + 1) * TC, j * LANES:(j + 1) * LANES]
    return jnp.concatenate(
        [jnp.concatenate([ref[j, pl.ds(t, BATCH, stride=PITCH), :] for j in range(n)], axis=-1)
         for t in range(TC)], axis=0)


def _to_batch_major(ref, val):
    n = ref.shape[0]
    for j in range(n):
        for t in range(TC):
            ref[j, pl.ds(t, BATCH, stride=PITCH), :] = val[t * BATCH:(t + 1) * BATCH, j * LANES:(j + 1) * LANES]
    return jnp.concatenate(
        [jnp.concatenate([ref[j, b * PITCH:b * PITCH + TC, :] for j in range(n)], axis=-1)
         for b in range(BATCH)], axis=0)


def _split(x):
    hi = x.astype(bf16)
    return hi, (x - hi.astype(f32)).astype(bf16)


def _mixer_kernel(x_ref, xn_ref, ng_ref, win_ref, wgk_ref, vec_ref, gn_ref, lam_ref, bd_ref, cd_ref,
                  wglu_ref, wa_ref, wx_ref, wout_ref, o_ref,
                  pa_ref, pb_ref, st_ref, s5_ref, rgh_ref, rgprev_ref, bu_ref, a_ref, xin_ref, y_ref,
                  pu_ref, px_ref, pg_ref, pys_ref, pyr_ref, hn_ref, wsc_ref):
    step = pl.program_id(0)

    @pl.when(step == 0)
    def _():
        wsc_ref[:, 0:OFF_S5] = win_ref[:, 0:OFF_S5]
        wsc_ref[:, OFF_S5:D_PROJ] = win_ref[:, SRC_S5:D_IN]
        st_ref[...] = jnp.zeros_like(st_ref)
        s5_ref[...] = jnp.zeros_like(s5_ref)
        rgh_ref[...] = jnp.zeros_like(rgh_ref)
        rgprev_ref[...] = jnp.zeros_like(rgprev_ref)
        hin = _rms(x_ref[...].reshape(ROWS, D_MODEL), ng_ref[...]).astype(bf16)
        pa_ref[...] = _dot(hin, wsc_ref[...])

    def vec(r):
        return vec_ref[r:r + 1, :]

    def _mixer_tile(p_ref, pn_ref):
        hn_ref[...] = _rms(xn_ref[...].reshape(ROWS, D_MODEL), ng_ref[...]).astype(bf16)
        pending = [(c, min(PROJ_CHUNK, D_PROJ - c)) for c in range(0, D_PROJ, PROJ_CHUNK)]

        def next_proj(count):
            for _ in range(min(count, len(pending))):
                c, w = pending.pop(0)
                pn_ref[:, c:c + w] = _dot(hn_ref[...], wsc_ref[:, c:c + w])

        def seg(off, width):
            return p_ref[:, off:off + width]

        def seg3(off, width):
            return p_ref[:, off:off + width].reshape(BATCH, TC, width)

        u = _to_time_major(pu_ref, seg(OFF_S5, S5_WIDTH))
        next_proj(P0)
        bu_ref[...] = _dot(u.astype(bf16), bd_ref[...])
        next_proj(P1)

        rx = _to_time_major(px_ref, seg(OFF_RGX, RG_WIDTH))
        prev = rgprev_ref[...]
        xc = rx * vec(V_CW0 + RG_CONV - 1) + vec(V_CB)
        for s in range(1, RG_CONV):
            shifted = jnp.concatenate([prev[(RG_CONV - 1 - s) * BATCH:, :], rx[:ROWS - s * BATCH, :]], axis=0)
            xc = xc + shifted * vec(V_CW0 + RG_CONV - 1 - s)
        rgprev_ref[...] = rx[ROWS - (RG_CONV - 1) * BATCH:, :]
        xcb = xc.astype(bf16)
        r = jax.nn.sigmoid(_dot(xcb, wa_ref[...]) + vec(V_BA))
        i = jax.nn.sigmoid(_dot(xcb, wx_ref[...]) + vec(V_BX))
        log_a = -RG_C * r * _softplus(-vec(V_LAM))
        a = jnp.exp(log_a)
        a_ref[...] = a
        xin_ref[...] = jnp.sqrt(-jnp.tanh(log_a) * (a * a + 1.0)) * (i * xc)
        next_proj(P2)

        n_s5 = S5_FLAT // LANES
        n_rg = RG_WIDTH // LANES
        lre = [jnp.broadcast_to(lam_ref[0:1, j * LANES:(j + 1) * LANES], (BATCH, LANES)) for j in range(n_s5)]
        lim = [jnp.broadcast_to(lam_ref[1:2, j * LANES:(j + 1) * LANES], (BATCH, LANES)) for j in range(n_s5)]

        def scan_step(t, carry):
            sre, sim, h = carry
            rows = slice(t * BATCH, (t + 1) * BATCH)
            nre, nim, nh = [], [], []
            for j in range(n_s5):
                c_re = slice(j * LANES, (j + 1) * LANES)
                c_im = slice(S5_FLAT + j * LANES, S5_FLAT + (j + 1) * LANES)
                re = lre[j] * sre[j] - lim[j] * sim[j] + bu_ref[rows, c_re]
                im = lre[j] * sim[j] + lim[j] * sre[j] + bu_ref[rows, c_im]
                bu_ref[rows, c_re] = re
                bu_ref[rows, c_im] = im
                nre.append(re)
                nim.append(im)
            for j in range(n_rg):
                cj = slice(j * LANES, (j + 1) * LANES)
                hj = a_ref[rows, cj] * h[j] + xin_ref[rows, cj]
                xin_ref[rows, cj] = hj
                nh.append(hj)
            return tuple(nre), tuple(nim), tuple(nh)

        carry = (tuple(s5_ref[j] for j in range(n_s5)),
                 tuple(s5_ref[n_s5 + j] for j in range(n_s5)),
                 tuple(rgh_ref[j] for j in range(n_rg)))
        for t in range(TC):
            carry = scan_step(t, carry)
            if t in SCAN_PROJ_AT:
                next_proj(1)
        sre, sim, h = carry
        for j in range(n_s5):
            s5_ref[j] = sre[j]
            s5_ref[n_s5 + j] = sim[j]
        for j in range(n_rg):
            rgh_ref[j] = h[j]

        ys = _dot(bu_ref[...].astype(bf16), cd_ref[...]) + vec(V_S5D) * u
        ys = jax.nn.gelu(ys)
        next_proj(P3)
        ys = ys * jax.nn.sigmoid(_dot(ys.astype(bf16), wglu_ref[...]) + vec(V_BGLU))
        y_ref[:, GLA_DV_TOTAL:GLA_DV_TOTAL + S5_WIDTH] = _to_batch_major(pys_ref, _rms(ys, vec(V_S5N))).astype(bf16)
        next_proj(P4)

        yr = xin_ref[...] * jax.nn.gelu(_to_time_major(pg_ref, seg(OFF_RGG, RG_WIDTH)))
        y_ref[:, GLA_DV_TOTAL + S5_WIDTH:] = _to_batch_major(pyr_ref, _rms(yr, vec(V_RGN))).astype(bf16)
        next_proj(P5)

        gl_hi, gl_lo = _split(seg(OFF_GKL, LANES))
        wg_hi, wg_lo = _split(wgk_ref[...])
        gk_pre = _dot(gl_hi, wg_hi) + _dot(gl_lo, wg_hi) + _dot(gl_hi, wg_lo) + vec(V_BGK)
        gk = (jnp.minimum(gk_pre, 0.0) - jnp.log1p(jnp.exp(-jnp.abs(gk_pre)))) / GLA_GATE_NORM
        causal = (lax.broadcasted_iota(jnp.int32, (TC, TC), 1)
                  <= lax.broadcasted_iota(jnp.int32, (TC, TC), 0))
        tril = jnp.where(causal, 1.0, 0.0).astype(bf16)
        gk_hi, gk_lo = _split(gk)
        bc = jnp.concatenate(
            [_dot(tril, gk_hi[b * TC:(b + 1) * TC, :]) + _dot(tril, gk_lo[b * TC:(b + 1) * TC, :])
             for b in range(BATCH)], axis=0)
        bc3 = bc.reshape(BATCH, TC, GLA_DK_TOTAL)
        bmid = bc3[:, TC // 2 - 1:TC // 2, :]
        blast = bc3[:, TC - 1:TC, :]
        q3 = seg3(OFF_Q, GLA_DK_TOTAL) * (GLA_DK ** -0.5)
        k3 = seg3(OFF_K, GLA_DK_TOTAL)
        qs3 = q3 * jnp.exp(bc3 - bmid)
        kt3 = (k3 * jnp.exp(bmid - bc3)).astype(bf16)
        qe3 = (q3 * jnp.exp(bc3)).astype(bf16)
        kl3 = (k3 * jnp.exp(blast - bc3)).astype(bf16)
        decay3 = jnp.exp(blast)
        next_proj(len(pending))

        lane_head = lax.broadcasted_iota(jnp.int32, (TC, GLA_DK_TOTAL), 1) >> DK_SHIFT
        gn = gn_ref[...]

        for b in range(BATCH):
            vb = p_ref[b * TC:(b + 1) * TC, OFF_V:OFF_V + GLA_DV_TOTAL].astype(bf16)
            qs = qs3[b]
            qstack = jnp.concatenate(
                [jnp.where(lane_head == hd, qs, 0.0) for hd in range(GLA_HEADS)], axis=0).astype(bf16)
            scores = _dot_nt(qstack, kt3[b])
            st_b = st_ref[b]
            o_inter = _dot_nt(qe3[b], st_b.astype(bf16))
            gb = p_ref[b * TC:(b + 1) * TC, OFF_G:OFF_G + GLA_DV_TOTAL]
            for hd in range(GLA_HEADS):
                sc = jnp.where(causal, scores[hd * TC:(hd + 1) * TC, :], 0.0).astype(bf16)
                cols = slice(hd * GLA_DV, (hd + 1) * GLA_DV)
                o = _dot(sc, vb[:, cols]) + o_inter[:, cols]
                g_h = gb[:, cols]
                y_ref[b * TC:(b + 1) * TC, cols] = (_rms(o, gn) * jax.nn.silu(g_h)).astype(bf16)
            klb = kl3[b]
            for hd in range(GLA_HEADS):
                rows_h = slice(hd * GLA_DV, (hd + 1) * GLA_DV)
                kv_h = _dot_tn(vb[:, rows_h], jnp.where(lane_head == hd, klb, jnp.zeros_like(klb)))
                st_ref[b, rows_h, :] = st_b[rows_h, :] * decay3[b] + kv_h

        out = _dot(y_ref[...], wout_ref[...])
        o_ref[...] = x_ref[...] + out.reshape(BATCH, TC, D_MODEL)

    @pl.when((step & 1) == 0)
    def _():
        _mixer_tile(pa_ref, pb_ref)

    @pl.when((step & 1) == 1)
    def _():
        _mixer_tile(pb_ref, pa_ref)


def _mixer(x3, attn_g, prm, layer):
    n_t = SEQ // TC

    def lay(shape):
        return pl.BlockSpec((None,) + shape, lambda i: (layer,) + (0,) * len(shape),
                            pipeline_mode=pl.Buffered(1))

    return pl.pallas_call(
        _mixer_kernel,
        grid=(n_t,),
        in_specs=[
            pl.BlockSpec((BATCH, TC, D_MODEL), lambda i: (0, i, 0)),
            pl.BlockSpec((BATCH, TC, D_MODEL), lambda i: (0, jnp.minimum(i + 1, n_t - 1), 0)),
            lay((1, D_MODEL)),
            lay((D_MODEL, D_IN)),
            lay((LANES, GLA_DK_TOTAL)),
            lay((VEC_ROWS, RG_WIDTH)),
            lay((1, GLA_DV)),
            lay((2, S5_FLAT)),
            lay((S5_WIDTH, 2 * S5_FLAT)),
            lay((2 * S5_FLAT, S5_WIDTH)),
            lay((S5_WIDTH, S5_WIDTH)),
            lay((RG_WIDTH, RG_WIDTH)),
            lay((RG_WIDTH, RG_WIDTH)),
            lay((D_MODEL, D_MODEL)),
        ],
        out_specs=pl.BlockSpec((BATCH, TC, D_MODEL), lambda i: (0, i, 0)),
        out_shape=jax.ShapeDtypeStruct((BATCH, SEQ, D_MODEL), f32),
        scratch_shapes=[
            pltpu.VMEM((ROWS, D_PROJ), f32),
            pltpu.VMEM((ROWS, D_PROJ), f32),
            pltpu.VMEM((BATCH, GLA_DV_TOTAL, GLA_DK_TOTAL), f32),
            pltpu.VMEM((2 * S5_FLAT // LANES, BATCH, LANES), f32),
            pltpu.VMEM((RG_WIDTH // LANES, BATCH, LANES), f32),
            pltpu.VMEM(((RG_CONV - 1) * BATCH, RG_WIDTH), f32),
            pltpu.VMEM((ROWS, 2 * S5_FLAT), f32),
            pltpu.VMEM((ROWS, RG_WIDTH), f32),
            pltpu.VMEM((ROWS, RG_WIDTH), f32),
            pltpu.VMEM((ROWS, D_MODEL), bf16),
        ] + [pltpu.VMEM((S5_WIDTH // LANES, BATCH * PITCH, LANES), f32)] * 5 + [
            pltpu.VMEM((ROWS, D_MODEL), bf16),
            pltpu.VMEM((D_MODEL, D_PROJ), bf16)],
        compiler_params=pltpu.CompilerParams(
            dimension_semantics=("arbitrary",), vmem_limit_bytes=VMEM_LIMIT),
        name="mixer",
    )(x3, x3, attn_g, prm["w_in"], prm["wgk"], prm["vec"], prm["gla_norm"], prm["lam_bar"], prm["bd"], prm["cd"],
      prm["w_glu"], prm["wa"], prm["wx"], prm["w_out"])


def _ffn_kernel(x_ref, g_ref, wup_ref, cw_ref, cb_ref, wdn_ref, fg_ref, o_ref, carry_ref, act_ref, *, last):
    @pl.when(pl.program_id(1) == 0)
    def _():
        carry_ref[...] = jnp.zeros_like(carry_ref)

    x = x_ref[...]
    h = _rms(x, g_ref[...]).astype(bf16)
    row = lax.broadcasted_iota(jnp.int32, (TM, FF_CHUNK), 0)
    for j in range(D_FF // FF_CHUNK):
        cols = slice(j * FF_CHUNK, (j + 1) * FF_CHUNK)
        up = _dot(h, wup_ref[:, cols])
        gv = _dot(h, wup_ref[:, D_FF + j * FF_CHUNK:D_FF + (j + 1) * FF_CHUNK])
        conv = up * cw_ref[FFN_CONV - 1:FFN_CONV, cols] + cb_ref[:, cols]
        for s in range(1, FFN_CONV):
            shifted = pltpu.roll(up, s, 0)
            for t in range(s):
                shifted = jnp.where(row == t, carry_ref[SUBLANES - s + t:SUBLANES - s + t + 1, cols], shifted)
            conv = conv + shifted * cw_ref[FFN_CONV - 1 - s:FFN_CONV - s, cols]
        carry_ref[:, cols] = up[TM - SUBLANES:, :]
        act_ref[:, cols] = (jax.nn.gelu(conv) * gv).astype(bf16)
    out = x + _dot(act_ref[...], wdn_ref[...])
    o_ref[...] = _rms(out, fg_ref[...]) if last else out


def _ffn(x3, mlp_norm, w_up, conv_w, conv_b, w_down, final_g, layer):
    def lay(shape):
        return pl.BlockSpec((None,) + shape, lambda b, i: (layer,) + (0,) * len(shape),
                            pipeline_mode=pl.Buffered(1))

    return pl.pallas_call(
        functools.partial(_ffn_kernel, last=(layer == DEPTH - 1)),
        grid=(BATCH, SEQ // TM),
        in_specs=[
            pl.BlockSpec((None, TM, D_MODEL), lambda b, i: (b, i, 0)),
            lay((1, D_MODEL)),
            lay((D_MODEL, 2 * D_FF)),
            lay((FFN_CONV, D_FF)),
            lay((1, D_FF)),
            lay((D_FF, D_MODEL)),
            pl.BlockSpec((1, D_MODEL), lambda b, i: (0, 0)),
        ],
        out_specs=pl.BlockSpec((None, TM, D_MODEL), lambda b, i: (b, i, 0)),
        out_shape=jax.ShapeDtypeStruct((BATCH, SEQ, D_MODEL), f32),
        scratch_shapes=[pltpu.VMEM((SUBLANES, D_FF), f32),
                        pltpu.VMEM((TM, D_FF), bf16)],
        compiler_params=pltpu.CompilerParams(
            dimension_semantics=("arbitrary", "arbitrary"), vmem_limit_bytes=VMEM_LIMIT),
        name="ffn",
    )(x3, mlp_norm, w_up, conv_w, conv_b, w_down, final_g)


def _block_diag(w):
    nl, nh, n, m = w.shape
    eye = jnp.eye(nh, dtype=w.dtype)
    return jnp.einsum('lhnm,hk->lhnkm', w, eye).reshape(nl, nh * n, nh * m)


def _prepare(w_in, gla_w_gk_up, gla_b_gk, gla_norm, s5_lambda_re, s5_lambda_im, s5_log_dt,
             s5_b_re, s5_b_im, s5_c_re, s5_c_im, s5_d, s5_w_glu, s5_b_glu, s5_norm,
             rg_conv_w, rg_conv_b, rg_w_a, rg_b_a, rg_w_x, rg_b_x, rg_lambda, rg_norm, w_out):
    nl = w_in.shape[0]
    wgk = jnp.concatenate(
        [gla_w_gk_up, jnp.zeros((nl, LANES - GLA_LOW_RANK, GLA_DK_TOTAL), f32)], axis=1)

    lam = lax.complex(s5_lambda_re, s5_lambda_im)
    dt = jnp.exp(s5_log_dt)[..., None]
    lam_bar = jnp.exp(lam * dt)
    b_bar = ((lam_bar - 1.0) / lam)[..., None] * lax.complex(s5_b_re, s5_b_im)
    bt = jnp.swapaxes(b_bar, 2, 3)
    bd = jnp.concatenate([_block_diag(bt.real), _block_diag(bt.imag)], axis=-1).astype(bf16)
    ct = jnp.swapaxes(lax.complex(s5_c_re, s5_c_im), 2, 3)
    cd = jnp.concatenate([_block_diag(ct.real), -_block_diag(ct.imag)], axis=1).astype(bf16)
    lam_flat = jnp.stack([lam_bar.real.reshape(nl, S5_FLAT), lam_bar.imag.reshape(nl, S5_FLAT)], axis=1)

    rows = [gla_b_gk, s5_d, s5_b_glu, s5_norm,
            rg_conv_w[:, 0], rg_conv_w[:, 1], rg_conv_w[:, 2], rg_conv_w[:, 3],
            rg_conv_b, rg_b_a, rg_b_x, rg_lambda, rg_norm]
    vec = jnp.stack(rows + [jnp.zeros_like(s5_d)] * (VEC_ROWS - len(rows)), axis=1)
    return dict(
        w_in=w_in.astype(bf16), wgk=wgk, vec=vec, gla_norm=gla_norm[:, None, :], lam_bar=lam_flat,
        bd=bd, cd=cd, w_glu=s5_w_glu.astype(bf16),
        wa=_block_diag(rg_w_a).astype(bf16), wx=_block_diag(rg_w_x).astype(bf16),
        w_out=w_out.astype(bf16))


def kernel(x, attn_norm, w_in, gla_w_gk_up, gla_b_gk, gla_norm, s5_lambda_re, s5_lambda_im, s5_log_dt, s5_b_re, s5_b_im, s5_c_re, s5_c_im, s5_d, s5_w_glu, s5_b_glu, s5_norm, rg_conv_w, rg_conv_b, rg_w_a, rg_b_a, rg_w_x, rg_b_x, rg_lambda, rg_norm, w_out, mlp_norm, w_up, mlp_conv_w, mlp_conv_b, w_down, final_norm):
    prm = _prepare(w_in, gla_w_gk_up, gla_b_gk, gla_norm, s5_lambda_re, s5_lambda_im, s5_log_dt,
                   s5_b_re, s5_b_im, s5_c_re, s5_c_im, s5_d, s5_w_glu, s5_b_glu, s5_norm,
                   rg_conv_w, rg_conv_b, rg_w_a, rg_b_a, rg_w_x, rg_b_x, rg_lambda, rg_norm, w_out)
    attn_g = attn_norm[:, None, :]
    mlp_g = mlp_norm[:, None, :]
    w_up_b = w_up.astype(bf16)
    w_down_b = w_down.astype(bf16)
    conv_b = mlp_conv_b[:, None, :]
    for layer in range(DEPTH):
        x = _mixer(x, attn_g, prm, layer)
        x = _ffn(x, mlp_g, w_up_b, mlp_conv_w, conv_b, w_down_b, final_norm[None, :], layer)
    return x
```

```python
import functools

import jax
import jax.numpy as jnp
from jax import lax
from jax.experimental import pallas as pl
from jax.experimental.pallas import tpu as pltpu

D_MODEL = 1024
BATCH = 8
SEQ = 2048
DEPTH = 4
EPS = 1e-6
GLA_HEADS = 4
GLA_DK_TOTAL = 256
GLA_DV_TOTAL = 512
GLA_DK = 64
GLA_DV = 128
GLA_LOW_RANK = 16
GLA_GATE_NORM = 16.0
GLA_CHUNK = 64
S5_WIDTH = 256
S5_GROUP = 16
S5_GROUPS = 16
S5_STATE = 64
S5_FLAT = S5_GROUPS * S5_STATE
RG_WIDTH = 256
RG_HEADS = 4
RG_HEAD_DIM = 64
RG_CONV = 4
RG_C = 8.0
D_FF = 2816
FFN_CONV = 3

LANES = 128
SUBLANES = 8
VMEM_LIMIT = 62 * 1024 * 1024

D_IN = 2320
OFF_Q = 0
OFF_K = 256
OFF_V = 512
OFF_G = 1024
OFF_GKL = 1536
SRC_S5 = OFF_GKL + GLA_LOW_RANK
OFF_S5 = OFF_GKL + LANES
OFF_RGX = OFF_S5 + S5_WIDTH
OFF_RGG = OFF_RGX + RG_WIDTH
D_PROJ = OFF_RGG + RG_WIDTH

TC = GLA_CHUNK
PITCH = TC + SUBLANES
ROWS = BATCH * TC
TM = 1024
FF_CHUNK = 256
PROJ_CHUNK = 256
P0, P1, P2, P3, P4, P5 = 1, 1, 1, 1, 1, 1
SCAN_PROJ_AT = (15, 37, 58)

DK_SHIFT = GLA_DK.bit_length() - 1

VEC_ROWS = 16
V_BGK, V_S5D, V_BGLU, V_S5N, V_CW0, V_CB, V_BA, V_BX, V_LAM, V_RGN = 0, 1, 2, 3, 4, 8, 9, 10, 11, 12

f32 = jnp.float32
bf16 = jnp.bfloat16


def _rms(x, g):
    return x * lax.rsqrt(jnp.mean(x * x, axis=-1, keepdims=True) + EPS) * g


def _softplus(x):
    return jnp.maximum(x, 0.0) + jnp.log1p(jnp.exp(-jnp.abs(x)))


def _dot(a, b):
    return jnp.dot(a, b, preferred_element_type=f32)


def _dot_nt(a, b):
    return lax.dot_general(a, b, (((1,), (1,)), ((), ())), preferred_element_type=f32)


def _dot_tn(a, b):
    return lax.dot_general(a, b, (((0,), (0,)), ((), ())), preferred_element_type=f32)


def _to_time_major(ref, val):
    n = ref.shape[0]
    for j in range(n):
        for b in range(BATCH):
            ref[j, b * PITCH:b * PITCH + TC, :] = val[b * TC:(b + 1) * TC, j * LANES:(j + 1) * LANES]
    return jnp.concatenate(
        [jnp.concatenate([ref[j, pl.ds(t, BATCH, stride=PITCH), :] for j in range(n)], axis=-1)
         for t in range(TC)], axis=0)


def _to_batch_major(ref, val):
    n = ref.shape[0]
    for j in range(n):
        for t in range(TC):
            ref[j, pl.ds(t, BATCH, stride=PITCH), :] = val[t * BATCH:(t + 1) * BATCH, j * LANES:(j + 1) * LANES]
    return jnp.concatenate(
        [jnp.concatenate([ref[j, b * PITCH:b * PITCH + TC, :] for j in range(n)], axis=-1)
         for b in range(BATCH)], axis=0)


def _split(x):
    hi = x.astype(bf16)
    return hi, (x - hi.astype(f32)).astype(bf16)


def _mixer_kernel(x_ref, xn_ref, ng_ref, win_ref, wgk_ref, vec_ref, gn_ref, lam_ref, bd_ref, cd_ref,
                  wglu_ref, wa_ref, wx_ref, wout_ref, o_ref,
                  pa_ref, pb_ref, st_ref, s5_ref, rgh_ref, rgprev_ref, bu_ref, a_ref, xin_ref, y_ref,
                  pu_ref, px_ref, pg_ref, pys_ref, pyr_ref, hn_ref, wsc_ref):
    step = pl.program_id(0)

    @pl.when(step == 0)
    def _():
        wsc_ref[:, 0:OFF_S5] = win_ref[:, 0:OFF_S5]
        wsc_ref[:, OFF_S5:D_PROJ] = win_ref[:, SRC_S5:D_IN]
        st_ref[...] = jnp.zeros_like(st_ref)
        s5_ref[...] = jnp.zeros_like(s5_ref)
        rgh_ref[...] = jnp.zeros_like(rgh_ref)
        rgprev_ref[...] = jnp.zeros_like(rgprev_ref)
        hin = _rms(x_ref[:, 0:TC, :].reshape(ROWS, D_MODEL), ng_ref[...]).astype(bf16)
        pa_ref[...] = _dot(hin, wsc_ref[...])

    def vec(r):
        return vec_ref[r:r + 1, :]

    def _mixer_tile(half, p_ref, pn_ref):
        cur = slice(half * TC, (half + 1) * TC)
        x_next = x_ref[:, TC:2 * TC, :] if half == 0 else xn_ref[...]
        hn_ref[...] = _rms(x_next.reshape(ROWS, D_MODEL), ng_ref[...]).astype(bf16)
        pending = [(c, min(PROJ_CHUNK, D_PROJ - c)) for c in range(0, D_PROJ, PROJ_CHUNK)]

        def next_proj(count):
            for _ in range(min(count, len(pending))):
                c, w = pending.pop(0)
                pn_ref[:, c:c + w] = _dot(hn_ref[...], wsc_ref[:, c:c + w])

        def seg(off, width):
            return p_ref[:, off:off + width]

        def seg3(off, width):
            return p_ref[:, off:off + width].reshape(BATCH, TC, width)

        u = _to_time_major(pu_ref, seg(OFF_S5, S5_WIDTH))
        next_proj(P0)
        bu_ref[...] = _dot(u.astype(bf16), bd_ref[...])
        next_proj(P1)

        rx = _to_time_major(px_ref, seg(OFF_RGX, RG_WIDTH))
        prev = rgprev_ref[...]
        xc = rx * vec(V_CW0 + RG_CONV - 1) + vec(V_CB)
        for s in range(1, RG_CONV):
            shifted = jnp.concatenate([prev[(RG_CONV - 1 - s) * BATCH:, :], rx[:ROWS - s * BATCH, :]], axis=0)
            xc = xc + shifted * vec(V_CW0 + RG_CONV - 1 - s)
        rgprev_ref[...] = rx[ROWS - (RG_CONV - 1) * BATCH:, :]
        xcb = xc.astype(bf16)
        r = jax.nn.sigmoid(_dot(xcb, wa_ref[...]) + vec(V_BA))
        i = jax.nn.sigmoid(_dot(xcb, wx_ref[...]) + vec(V_BX))
        log_a = -RG_C * r * _softplus(-vec(V_LAM))
        a = jnp.exp(log_a)
        a_ref[...] = a
        xin_ref[...] = jnp.sqrt(-jnp.tanh(log_a) * (a * a + 1.0)) * (i * xc)
        next_proj(P2)

        n_s5 = S5_FLAT // LANES
        n_rg = RG_WIDTH // LANES
        lre = [jnp.broadcast_to(lam_ref[0:1, j * LANES:(j + 1) * LANES], (BATCH, LANES)) for j in range(n_s5)]
        lim = [jnp.broadcast_to(lam_ref[1:2, j * LANES:(j + 1) * LANES], (BATCH, LANES)) for j in range(n_s5)]

        def scan_step(t, carry):
            sre, sim, h = carry
            rows = slice(t * BATCH, (t + 1) * BATCH)
            nre, nim, nh = [], [], []
            for j in range(n_s5):
                c_re = slice(j * LANES, (j + 1) * LANES)
                c_im = slice(S5_FLAT + j * LANES, S5_FLAT + (j + 1) * LANES)
                re = lre[j] * sre[j] - lim[j] * sim[j] + bu_ref[rows, c_re]
                im = lre[j] * sim[j] + lim[j] * sre[j] + bu_ref[rows, c_im]
                bu_ref[rows, c_re] = re
                bu_ref[rows, c_im] = im
                nre.append(re)
                nim.append(im)
            for j in range(n_rg):
                cj = slice(j * LANES, (j + 1) * LANES)
                hj = a_ref[rows, cj] * h[j] + xin_ref[rows, cj]
                xin_ref[rows, cj] = hj
                nh.append(hj)
            return tuple(nre), tuple(nim), tuple(nh)

        carry = (tuple(s5_ref[j] for j in range(n_s5)),
                 tuple(s5_ref[n_s5 + j] for j in range(n_s5)),
                 tuple(rgh_ref[j] for j in range(n_rg)))
        for t in range(TC):
            carry = scan_step(t, carry)
            if t in SCAN_PROJ_AT:
                next_proj(1)
        sre, sim, h = carry
        for j in range(n_s5):
            s5_ref[j] = sre[j]
            s5_ref[n_s5 + j] = sim[j]
        for j in range(n_rg):
            rgh_ref[j] = h[j]

        ys = _dot(bu_ref[...].astype(bf16), cd_ref[...]) + vec(V_S5D) * u
        ys = jax.nn.gelu(ys)
        next_proj(P3)
        ys = ys * jax.nn.sigmoid(_dot(ys.astype(bf16), wglu_ref[...]) + vec(V_BGLU))
        y_ref[:, GLA_DV_TOTAL:GLA_DV_TOTAL + S5_WIDTH] = _to_batch_major(pys_ref, _rms(ys, vec(V_S5N))).astype(bf16)
        next_proj(P4)

        yr = xin_ref[...] * jax.nn.gelu(_to_time_major(pg_ref, seg(OFF_RGG, RG_WIDTH)))
        y_ref[:, GLA_DV_TOTAL + S5_WIDTH:] = _to_batch_major(pyr_ref, _rms(yr, vec(V_RGN))).astype(bf16)
        next_proj(P5)

        gl_hi, gl_lo = _split(seg(OFF_GKL, LANES))
        wg_hi, wg_lo = _split(wgk_ref[...])
        gk_pre = _dot(gl_hi, wg_hi) + _dot(gl_lo, wg_hi) + _dot(gl_hi, wg_lo) + vec(V_BGK)
        gk = (jnp.minimum(gk_pre, 0.0) - jnp.log1p(jnp.exp(-jnp.abs(gk_pre)))) / GLA_GATE_NORM
        causal = (lax.broadcasted_iota(jnp.int32, (TC, TC), 1)
                  <= lax.broadcasted_iota(jnp.int32, (TC, TC), 0))
        tril = jnp.where(causal, 1.0, 0.0).astype(bf16)
        gk_hi, gk_lo = _split(gk)
        bc = jnp.concatenate(
            [_dot(tril, gk_hi[b * TC:(b + 1) * TC, :]) + _dot(tril, gk_lo[b * TC:(b + 1) * TC, :])
             for b in range(BATCH)], axis=0)
        bc3 = bc.reshape(BATCH, TC, GLA_DK_TOTAL)
        bmid = bc3[:, TC // 2 - 1:TC // 2, :]
        blast = bc3[:, TC - 1:TC, :]
        q3 = seg3(OFF_Q, GLA_DK_TOTAL) * (GLA_DK ** -0.5)
        k3 = seg3(OFF_K, GLA_DK_TOTAL)
        qs3 = q3 * jnp.exp(bc3 - bmid)
        kt3 = (k3 * jnp.exp(bmid - bc3)).astype(bf16)
        qe3 = (q3 * jnp.exp(bc3)).astype(bf16)
        kl3 = (k3 * jnp.exp(blast - bc3)).astype(bf16)
        decay3 = jnp.exp(blast)
        next_proj(len(pending))

        lane_head = lax.broadcasted_iota(jnp.int32, (TC, GLA_DK_TOTAL), 1) >> DK_SHIFT
        gn = gn_ref[...]

        for b in range(BATCH):
            vb = p_ref[b * TC:(b + 1) * TC, OFF_V:OFF_V + GLA_DV_TOTAL].astype(bf16)
            qs = qs3[b]
            qstack = jnp.concatenate(
                [jnp.where(lane_head == hd, qs, 0.0) for hd in range(GLA_HEADS)], axis=0).astype(bf16)
            scores = _dot_nt(qstack, kt3[b])
            st_b = st_ref[b]
            o_inter = _dot_nt(qe3[b], st_b.astype(bf16))
            gb = p_ref[b * TC:(b + 1) * TC, OFF_G:OFF_G + GLA_DV_TOTAL]
            for hd in range(GLA_HEADS):
                sc = jnp.where(causal, scores[hd * TC:(hd + 1) * TC, :], 0.0).astype(bf16)
                cols = slice(hd * GLA_DV, (hd + 1) * GLA_DV)
                o = _dot(sc, vb[:, cols]) + o_inter[:, cols]
                g_h = gb[:, cols]
                y_ref[b * TC:(b + 1) * TC, cols] = (_rms(o, gn) * jax.nn.silu(g_h)).astype(bf16)
            klb = kl3[b]
            for hd in range(GLA_HEADS):
                rows_h = slice(hd * GLA_DV, (hd + 1) * GLA_DV)
                kv_h = _dot_tn(vb[:, rows_h], jnp.where(lane_head == hd, klb, jnp.zeros_like(klb)))
                st_ref[b, rows_h, :] = st_b[rows_h, :] * decay3[b] + kv_h

        out = _dot(y_ref[...], wout_ref[...])
        o_ref[:, cur, :] = x_ref[:, cur, :] + out.reshape(BATCH, TC, D_MODEL)

    _mixer_tile(0, pa_ref, pb_ref)
    _mixer_tile(1, pb_ref, pa_ref)


def _mixer(x3, attn_g, prm, layer):
    n_t = SEQ // (2 * TC)

    def lay(shape):
        return pl.BlockSpec((None,) + shape, lambda i: (layer,) + (0,) * len(shape),
                            pipeline_mode=pl.Buffered(1))

    return pl.pallas_call(
        _mixer_kernel,
        grid=(n_t,),
        in_specs=[
            pl.BlockSpec((BATCH, 2 * TC, D_MODEL), lambda i: (0, i, 0)),
            pl.BlockSpec((BATCH, TC, D_MODEL), lambda i: (0, jnp.minimum(2 * i + 2, 2 * n_t - 1), 0)),
            lay((1, D_MODEL)),
            lay((D_MODEL, D_IN)),
            lay((LANES, GLA_DK_TOTAL)),
            lay((VEC_ROWS, RG_WIDTH)),
            lay((1, GLA_DV)),
            lay((2, S5_FLAT)),
            lay((S5_WIDTH, 2 * S5_FLAT)),
            lay((2 * S5_FLAT, S5_WIDTH)),
            lay((S5_WIDTH, S5_WIDTH)),
            lay((RG_WIDTH, RG_WIDTH)),
            lay((RG_WIDTH, RG_WIDTH)),
            lay((D_MODEL, D_MODEL)),
        ],
        out_specs=pl.BlockSpec((BATCH, 2 * TC, D_MODEL), lambda i: (0, i, 0)),
        out_shape=jax.ShapeDtypeStruct((BATCH, SEQ, D_MODEL), f32),
        scratch_shapes=[
            pltpu.VMEM((ROWS, D_PROJ), f32),
            pltpu.VMEM((ROWS, D_PROJ), f32),
            pltpu.VMEM((BATCH, GLA_DV_TOTAL, GLA_DK_TOTAL), f32),
            pltpu.VMEM((2 * S5_FLAT // LANES, BATCH, LANES), f32),
            pltpu.VMEM((RG_WIDTH // LANES, BATCH, LANES), f32),
            pltpu.VMEM(((RG_CONV - 1) * BATCH, RG_WIDTH), f32),
            pltpu.VMEM((ROWS, 2 * S5_FLAT), f32),
            pltpu.VMEM((ROWS, RG_WIDTH), f32),
            pltpu.VMEM((ROWS, RG_WIDTH), f32),
            pltpu.VMEM((ROWS, D_MODEL), bf16),
        ] + [pltpu.VMEM((S5_WIDTH // LANES, BATCH * PITCH, LANES), f32)] * 5 + [
            pltpu.VMEM((ROWS, D_MODEL), bf16),
            pltpu.VMEM((D_MODEL, D_PROJ), bf16)],
        compiler_params=pltpu.CompilerParams(
            dimension_semantics=("arbitrary",), vmem_limit_bytes=VMEM_LIMIT),
        name="mixer",
    )(x3, x3, attn_g, prm["w_in"], prm["wgk"], prm["vec"], prm["gla_norm"], prm["lam_bar"], prm["bd"], prm["cd"],
      prm["w_glu"], prm["wa"], prm["wx"], prm["w_out"])


def _ffn_kernel(x_ref, g_ref, wup_ref, cw_ref, cb_ref, wdn_ref, fg_ref, o_ref, carry_ref, act_ref, *, last):
    @pl.when(pl.program_id(1) == 0)
    def _():
        carry_ref[...] = jnp.zeros_like(carry_ref)

    x = x_ref[...]
    h = _rms(x, g_ref[...]).astype(bf16)
    row = lax.broadcasted_iota(jnp.int32, (TM, FF_CHUNK), 0)
    for j in range(D_FF // FF_CHUNK):
        cols = slice(j * FF_CHUNK, (j + 1) * FF_CHUNK)
        up = _dot(h, wup_ref[:, cols])
        gv = _dot(h, wup_ref[:, D_FF + j * FF_CHUNK:D_FF + (j + 1) * FF_CHUNK])
        conv = up * cw_ref[FFN_CONV - 1:FFN_CONV, cols] + cb_ref[:, cols]
        for s in range(1, FFN_CONV):
            shifted = pltpu.roll(up, s, 0)
            for t in range(s):
                shifted = jnp.where(row == t, carry_ref[SUBLANES - s + t:SUBLANES - s + t + 1, cols], shifted)
            conv = conv + shifted * cw_ref[FFN_CONV - 1 - s:FFN_CONV - s, cols]
        carry_ref[:, cols] = up[TM - SUBLANES:, :]
        act_ref[:, cols] = (jax.nn.gelu(conv) * gv).astype(bf16)
    out = x + _dot(act_ref[...], wdn_ref[...])
    o_ref[...] = _rms(out, fg_ref[...]) if last else out


def _ffn(x3, mlp_norm, w_up, conv_w, conv_b, w_down, final_g, layer):
    def lay(shape):
        return pl.BlockSpec((None,) + shape, lambda b, i: (layer,) + (0,) * len(shape),
                            pipeline_mode=pl.Buffered(1))

    return pl.pallas_call(
        functools.partial(_ffn_kernel, last=(layer == DEPTH - 1)),
        grid=(BATCH, SEQ // TM),
        in_specs=[
            pl.BlockSpec((None, TM, D_MODEL), lambda b, i: (b, i, 0)),
            lay((1, D_MODEL)),
            lay((D_MODEL, 2 * D_FF)),
            lay((FFN_CONV, D_FF)),
            lay((1, D_FF)),
            lay((D_FF, D_MODEL)),
            pl.BlockSpec((1, D_MODEL), lambda b, i: (0, 0)),
        ],
        out_specs=pl.BlockSpec((None, TM, D_MODEL), lambda b, i: (b, i, 0)),
        out_shape=jax.ShapeDtypeStruct((BATCH, SEQ, D_MODEL), f32),
        scratch_shapes=[pltpu.VMEM((SUBLANES, D_FF), f32),
                        pltpu.VMEM((TM, D_FF), bf16)],
        compiler_params=pltpu.CompilerParams(
            dimension_semantics=("arbitrary", "arbitrary"), vmem_limit_bytes=VMEM_LIMIT),
        name="ffn",
    )(x3, mlp_norm, w_up, conv_w, conv_b, w_down, final_g)


def _block_diag(w):
    nl, nh, n, m = w.shape
    eye = jnp.eye(nh, dtype=w.dtype)
    return jnp.einsum('lhnm,hk->lhnkm', w, eye).reshape(nl, nh * n, nh * m)


def _prepare(w_in, gla_w_gk_up, gla_b_gk, gla_norm, s5_lambda_re, s5_lambda_im, s5_log_dt,
             s5_b_re, s5_b_im, s5_c_re, s5_c_im, s5_d, s5_w_glu, s5_b_glu, s5_norm,
             rg_conv_w, rg_conv_b, rg_w_a, rg_b_a, rg_w_x, rg_b_x, rg_lambda, rg_norm, w_out):
    nl = w_in.shape[0]
    wgk = jnp.concatenate(
        [gla_w_gk_up, jnp.zeros((nl, LANES - GLA_LOW_RANK, GLA_DK_TOTAL), f32)], axis=1)

    lam = lax.complex(s5_lambda_re, s5_lambda_im)
    dt = jnp.exp(s5_log_dt)[..., None]
    lam_bar = jnp.exp(lam * dt)
    b_bar = ((lam_bar - 1.0) / lam)[..., None] * lax.complex(s5_b_re, s5_b_im)
    bt = jnp.swapaxes(b_bar, 2, 3)
    bd = jnp.concatenate([_block_diag(bt.real), _block_diag(bt.imag)], axis=-1).astype(bf16)
    ct = jnp.swapaxes(lax.complex(s5_c_re, s5_c_im), 2, 3)
    cd = jnp.concatenate([_block_diag(ct.real), -_block_diag(ct.imag)], axis=1).astype(bf16)
    lam_flat = jnp.stack([lam_bar.real.reshape(nl, S5_FLAT), lam_bar.imag.reshape(nl, S5_FLAT)], axis=1)

    rows = [gla_b_gk, s5_d, s5_b_glu, s5_norm,
            rg_conv_w[:, 0], rg_conv_w[:, 1], rg_conv_w[:, 2], rg_conv_w[:, 3],
            rg_conv_b, rg_b_a, rg_b_x, rg_lambda, rg_norm]
    vec = jnp.stack(rows + [jnp.zeros_like(s5_d)] * (VEC_ROWS - len(rows)), axis=1)
    return dict(
        w_in=w_in.astype(bf16), wgk=wgk, vec=vec, gla_norm=gla_norm[:, None, :], lam_bar=lam_flat,
        bd=bd, cd=cd, w_glu=s5_w_glu.astype(bf16),
        wa=_block_diag(rg_w_a).astype(bf16), wx=_block_diag(rg_w_x).astype(bf16),
        w_out=w_out.astype(bf16))


def kernel(x, attn_norm, w_in, gla_w_gk_up, gla_b_gk, gla_norm, s5_lambda_re, s5_lambda_im, s5_log_dt, s5_b_re, s5_b_im, s5_c_re, s5_c_im, s5_d, s5_w_glu, s5_b_glu, s5_norm, rg_conv_w, rg_conv_b, rg_w_a, rg_b_a, rg_w_x, rg_b_x, rg_lambda, rg_norm, w_out, mlp_norm, w_up, mlp_conv_w, mlp_conv_b, w_down, final_norm):
    prm = _prepare(w_in, gla_w_gk_up, gla_b_gk, gla_norm, s5_lambda_re, s5_lambda_im, s5_log_dt,
                   s5_b_re, s5_b_im, s5_c_re, s5_c_im, s5_d, s5_w_glu, s5_b_glu, s5_norm,
                   rg_conv_w, rg_conv_b, rg_w_a, rg_b_a, rg_w_x, rg_b_x, rg_lambda, rg_norm, w_out)
    attn_g = attn_norm[:, None, :]
    mlp_g = mlp_norm[:, None, :]
    w_up_b = w_up.astype(bf16)
    w_down_b = w_down.astype(bf16)
    conv_b = mlp_conv_b[:, None, :]
    for layer in range(DEPTH):
        x = _mixer(x, attn_g, prm, layer)
        x = _ffn(x, mlp_g, w_up_b, mlp_conv_w, conv_b, w_down_b, final_norm[None, :], layer)
    return x
```

```python
import functools

import jax
import jax.numpy as jnp
from jax import lax
from jax.experimental import pallas as pl
from jax.experimental.pallas import tpu as pltpu

D_MODEL = 1024
BATCH = 8
SEQ = 2048
DEPTH = 4
EPS = 1e-6
GLA_HEADS = 4
GLA_DK_TOTAL = 256
GLA_DV_TOTAL = 512
GLA_DK = 64
GLA_DV = 128
GLA_LOW_RANK = 16
GLA_GATE_NORM = 16.0
GLA_CHUNK = 64
S5_WIDTH = 256
S5_GROUP = 16
S5_GROUPS = 16
S5_STATE = 64
S5_FLAT = S5_GROUPS * S5_STATE
RG_WIDTH = 256
RG_HEADS = 4
RG_HEAD_DIM = 64
RG_CONV = 4
RG_C = 8.0
D_FF = 2816
FFN_CONV = 3

LANES = 128
SUBLANES = 8
VMEM_LIMIT = 62 * 1024 * 1024

D_IN = 2320
OFF_Q = 0
OFF_K = 256
OFF_V = 512
OFF_G = 1024
OFF_GKL = 1536
SRC_S5 = OFF_GKL + GLA_LOW_RANK
OFF_S5 = OFF_GKL + LANES
OFF_RGX = OFF_S5 + S5_WIDTH
OFF_RGG = OFF_RGX + RG_WIDTH
D_PROJ = OFF_RGG + RG_WIDTH

TC = GLA_CHUNK
PITCH = TC + SUBLANES
ROWS = BATCH * TC
TM = 1024
WIN_ROWS = 256
FF_CHUNK = 256
PROJ_CHUNK = 256
P0, P1, P2, P3, P4, P5 = 1, 1, 1, 1, 1, 1
SCAN_PROJ_AT = (15, 37, 58)

DK_SHIFT = GLA_DK.bit_length() - 1

VEC_ROWS = 16
V_BGK, V_S5D, V_BGLU, V_S5N, V_CW0, V_CB, V_BA, V_BX, V_LAM, V_RGN = 0, 1, 2, 3, 4, 8, 9, 10, 11, 12

f32 = jnp.float32
bf16 = jnp.bfloat16


def _rms(x, g):
    return x * lax.rsqrt(jnp.mean(x * x, axis=-1, keepdims=True) + EPS) * g


def _softplus(x):
    return jnp.maximum(x, 0.0) + jnp.log1p(jnp.exp(-jnp.abs(x)))


def _dot(a, b):
    return jnp.dot(a, b, preferred_element_type=f32)


def _dot_nt(a, b):
    return lax.dot_general(a, b, (((1,), (1,)), ((), ())), preferred_element_type=f32)


def _dot_tn(a, b):
    return lax.dot_general(a, b, (((0,), (0,)), ((), ())), preferred_element_type=f32)


def _win_kernel(w_ref, o_ref):
    w = w_ref[...]
    o_ref[:, 0:OFF_S5] = w[:, 0:OFF_S5].astype(bf16)
    o_ref[:, OFF_S5:D_PROJ] = w[:, SRC_S5:D_IN].astype(bf16)


def _win_layout(w_in):
    nl = w_in.shape[0]
    return pl.pallas_call(
        _win_kernel,
        grid=(nl, D_MODEL // WIN_ROWS),
        in_specs=[pl.BlockSpec((None, WIN_ROWS, D_IN), lambda l, i: (l, i, 0))],
        out_specs=pl.BlockSpec((None, WIN_ROWS, D_PROJ), lambda l, i: (l, i, 0)),
        out_shape=jax.ShapeDtypeStruct((nl, D_MODEL, D_PROJ), bf16),
        compiler_params=pltpu.CompilerParams(dimension_semantics=("arbitrary", "arbitrary")),
        name="win_layout",
    )(w_in)


def _to_time_major(ref, val):
    n = ref.shape[0]
    for j in range(n):
        for b in range(BATCH):
            ref[j, b * PITCH:b * PITCH + TC, :] = val[b * TC:(b + 1) * TC, j * LANES:(j + 1) * LANES]
    return jnp.concatenate(
        [jnp.concatenate([ref[j, pl.ds(t, BATCH, stride=PITCH), :] for j in range(n)], axis=-1)
         for t in range(TC)], axis=0)


def _to_batch_major(ref, val):
    n = ref.shape[0]
    for j in range(n):
        for t in range(TC):
            ref[j, pl.ds(t, BATCH, stride=PITCH), :] = val[t * BATCH:(t + 1) * BATCH, j * LANES:(j + 1) * LANES]
    return jnp.concatenate(
        [jnp.concatenate([ref[j, b * PITCH:b * PITCH + TC, :] for j in range(n)], axis=-1)
         for b in range(BATCH)], axis=0)


def _split(x):
    hi = x.astype(bf16)
    return hi, (x - hi.astype(f32)).astype(bf16)


def _mixer_kernel(x_ref, xn_ref, ng_ref, win_ref, wgk_ref, vec_ref, gn_ref, lam_ref, bd_ref, cd_ref,
                  wglu_ref, wa_ref, wx_ref, wout_ref, o_ref,
                  pa_ref, pb_ref, st_ref, s5_ref, rgh_ref, rgprev_ref, bu_ref, a_ref, xin_ref, y_ref,
                  pu_ref, px_ref, pg_ref, pys_ref, pyr_ref, hn_ref):
    step = pl.program_id(0)

    @pl.when(step == 0)
    def _():
        st_ref[...] = jnp.zeros_like(st_ref)
        s5_ref[...] = jnp.zeros_like(s5_ref)
        rgh_ref[...] = jnp.zeros_like(rgh_ref)
        rgprev_ref[...] = jnp.zeros_like(rgprev_ref)
        hin = _rms(x_ref[:, 0:TC, :].reshape(ROWS, D_MODEL), ng_ref[...]).astype(bf16)
        pa_ref[...] = _dot(hin, win_ref[...])

    def vec(r):
        return vec_ref[r:r + 1, :]

    def _mixer_tile(half, p_ref, pn_ref):
        cur = slice(half * TC, (half + 1) * TC)
        x_next = x_ref[:, TC:2 * TC, :] if half == 0 else xn_ref[...]
        hn_ref[...] = _rms(x_next.reshape(ROWS, D_MODEL), ng_ref[...]).astype(bf16)
        pending = [(c, min(PROJ_CHUNK, D_PROJ - c)) for c in range(0, D_PROJ, PROJ_CHUNK)]

        def next_proj(count):
            for _ in range(min(count, len(pending))):
                c, w = pending.pop(0)
                pn_ref[:, c:c + w] = _dot(hn_ref[...], win_ref[:, c:c + w])

        def seg(off, width):
            return p_ref[:, off:off + width]

        def seg3(off, width):
            return p_ref[:, off:off + width].reshape(BATCH, TC, width)

        u = _to_time_major(pu_ref, seg(OFF_S5, S5_WIDTH))
        next_proj(P0)
        bu_ref[...] = _dot(u.astype(bf16), bd_ref[...])
        next_proj(P1)

        rx = _to_time_major(px_ref, seg(OFF_RGX, RG_WIDTH))
        prev = rgprev_ref[...]
        xc = rx * vec(V_CW0 + RG_CONV - 1) + vec(V_CB)
        for s in range(1, RG_CONV):
            shifted = jnp.concatenate([prev[(RG_CONV - 1 - s) * BATCH:, :], rx[:ROWS - s * BATCH, :]], axis=0)
            xc = xc + shifted * vec(V_CW0 + RG_CONV - 1 - s)
        rgprev_ref[...] = rx[ROWS - (RG_CONV - 1) * BATCH:, :]
        xcb = xc.astype(bf16)
        r = jax.nn.sigmoid(_dot(xcb, wa_ref[...]) + vec(V_BA))
        i = jax.nn.sigmoid(_dot(xcb, wx_ref[...]) + vec(V_BX))
        log_a = -RG_C * r * _softplus(-vec(V_LAM))
        a = jnp.exp(log_a)
        a_ref[...] = a
        xin_ref[...] = jnp.sqrt(-jnp.tanh(log_a) * (a * a + 1.0)) * (i * xc)
        next_proj(P2)

        n_s5 = S5_FLAT // LANES
        n_rg = RG_WIDTH // LANES
        lre = [jnp.broadcast_to(lam_ref[0:1, j * LANES:(j + 1) * LANES], (BATCH, LANES)) for j in range(n_s5)]
        lim = [jnp.broadcast_to(lam_ref[1:2, j * LANES:(j + 1) * LANES], (BATCH, LANES)) for j in range(n_s5)]

        def scan_step(t, carry):
            sre, sim, h = carry
            rows = slice(t * BATCH, (t + 1) * BATCH)
            nre, nim, nh = [], [], []
            for j in range(n_s5):
                c_re = slice(j * LANES, (j + 1) * LANES)
                c_im = slice(S5_FLAT + j * LANES, S5_FLAT + (j + 1) * LANES)
                re = lre[j] * sre[j] - lim[j] * sim[j] + bu_ref[rows, c_re]
                im = lre[j] * sim[j] + lim[j] * sre[j] + bu_ref[rows, c_im]
                bu_ref[rows, c_re] = re
                bu_ref[rows, c_im] = im
                nre.append(re)
                nim.append(im)
            for j in range(n_rg):
                cj = slice(j * LANES, (j + 1) * LANES)
                hj = a_ref[rows, cj] * h[j] + xin_ref[rows, cj]
                xin_ref[rows, cj] = hj
                nh.append(hj)
            return tuple(nre), tuple(nim), tuple(nh)

        carry = (tuple(s5_ref[j] for j in range(n_s5)),
                 tuple(s5_ref[n_s5 + j] for j in range(n_s5)),
                 tuple(rgh_ref[j] for j in range(n_rg)))
        for t in range(TC):
            carry = scan_step(t, carry)
            if t in SCAN_PROJ_AT:
                next_proj(1)
        sre, sim, h = carry
        for j in range(n_s5):
            s5_ref[j] = sre[j]
            s5_ref[n_s5 + j] = sim[j]
        for j in range(n_rg):
            rgh_ref[j] = h[j]

        ys = _dot(bu_ref[...].astype(bf16), cd_ref[...]) + vec(V_S5D) * u
        ys = jax.nn.gelu(ys)
        next_proj(P3)
        ys = ys * jax.nn.sigmoid(_dot(ys.astype(bf16), wglu_ref[...]) + vec(V_BGLU))
        y_ref[:, GLA_DV_TOTAL:GLA_DV_TOTAL + S5_WIDTH] = _to_batch_major(pys_ref, _rms(ys, vec(V_S5N))).astype(bf16)
        next_proj(P4)

        yr = xin_ref[...] * jax.nn.gelu(_to_time_major(pg_ref, seg(OFF_RGG, RG_WIDTH)))
        y_ref[:, GLA_DV_TOTAL + S5_WIDTH:] = _to_batch_major(pyr_ref, _rms(yr, vec(V_RGN))).astype(bf16)
        next_proj(P5)

        gl_hi, gl_lo = _split(seg(OFF_GKL, LANES))
        wg_hi, wg_lo = _split(wgk_ref[...])
        gk_pre = _dot(gl_hi, wg_hi) + _dot(gl_lo, wg_hi) + _dot(gl_hi, wg_lo) + vec(V_BGK)
        gk = (jnp.minimum(gk_pre, 0.0) - jnp.log1p(jnp.exp(-jnp.abs(gk_pre)))) / GLA_GATE_NORM
        causal = (lax.broadcasted_iota(jnp.int32, (TC, TC), 1)
                  <= lax.broadcasted_iota(jnp.int32, (TC, TC), 0))
        tril = jnp.where(causal, 1.0, 0.0).astype(bf16)
        gk_hi, gk_lo = _split(gk)
        bc = jnp.concatenate(
            [_dot(tril, gk_hi[b * TC:(b + 1) * TC, :]) + _dot(tril, gk_lo[b * TC:(b + 1) * TC, :])
             for b in range(BATCH)], axis=0)
        bc3 = bc.reshape(BATCH, TC, GLA_DK_TOTAL)
        bmid = bc3[:, TC // 2 - 1:TC // 2, :]
        blast = bc3[:, TC - 1:TC, :]
        q3 = seg3(OFF_Q, GLA_DK_TOTAL) * (GLA_DK ** -0.5)
        k3 = seg3(OFF_K, GLA_DK_TOTAL)
        qs3 = q3 * jnp.exp(bc3 - bmid)
        kt3 = (k3 * jnp.exp(bmid - bc3)).astype(bf16)
        qe3 = (q3 * jnp.exp(bc3)).astype(bf16)
        kl3 = (k3 * jnp.exp(blast - bc3)).astype(bf16)
        decay3 = jnp.exp(blast)
        next_proj(len(pending))

        lane_head = lax.broadcasted_iota(jnp.int32, (TC, GLA_DK_TOTAL), 1) >> DK_SHIFT
        gn = gn_ref[...]

        for b in range(BATCH):
            vb = p_ref[b * TC:(b + 1) * TC, OFF_V:OFF_V + GLA_DV_TOTAL].astype(bf16)
            qs = qs3[b]
            qstack = jnp.concatenate(
                [jnp.where(lane_head == hd, qs, 0.0) for hd in range(GLA_HEADS)], axis=0).astype(bf16)
            scores = _dot_nt(qstack, kt3[b])
            st_b = st_ref[b]
            o_inter = _dot_nt(qe3[b], st_b.astype(bf16))
            gb = p_ref[b * TC:(b + 1) * TC, OFF_G:OFF_G + GLA_DV_TOTAL]
            for hd in range(GLA_HEADS):
                sc = jnp.where(causal, scores[hd * TC:(hd + 1) * TC, :], 0.0).astype(bf16)
                cols = slice(hd * GLA_DV, (hd + 1) * GLA_DV)
                o = _dot(sc, vb[:, cols]) + o_inter[:, cols]
                g_h = gb[:, cols]
                y_ref[b * TC:(b + 1) * TC, cols] = (_rms(o, gn) * jax.nn.silu(g_h)).astype(bf16)
            klb = kl3[b]
            for hd in range(GLA_HEADS):
                rows_h = slice(hd * GLA_DV, (hd + 1) * GLA_DV)
                kv_h = _dot_tn(vb[:, rows_h], jnp.where(lane_head == hd, klb, jnp.zeros_like(klb)))
                st_ref[b, rows_h, :] = st_b[rows_h, :] * decay3[b] + kv_h

        out = _dot(y_ref[...], wout_ref[...])
        o_ref[:, cur, :] = x_ref[:, cur, :] + out.reshape(BATCH, TC, D_MODEL)

    _mixer_tile(0, pa_ref, pb_ref)
    _mixer_tile(1, pb_ref, pa_ref)


def _mixer(x3, attn_g, prm, layer):
    n_t = SEQ // (2 * TC)

    def lay(shape):
        return pl.BlockSpec((None,) + shape, lambda i: (layer,) + (0,) * len(shape),
                            pipeline_mode=pl.Buffered(1))

    return pl.pallas_call(
        _mixer_kernel,
        grid=(n_t,),
        in_specs=[
            pl.BlockSpec((BATCH, 2 * TC, D_MODEL), lambda i: (0, i, 0)),
            pl.BlockSpec((BATCH, TC, D_MODEL), lambda i: (0, jnp.minimum(2 * i + 2, 2 * n_t - 1), 0)),
            lay((1, D_MODEL)),
            lay((D_MODEL, D_PROJ)),
            lay((LANES, GLA_DK_TOTAL)),
            lay((VEC_ROWS, RG_WIDTH)),
            lay((1, GLA_DV)),
            lay((2, S5_FLAT)),
            lay((S5_WIDTH, 2 * S5_FLAT)),
            lay((2 * S5_FLAT, S5_WIDTH)),
            lay((S5_WIDTH, S5_WIDTH)),
            lay((RG_WIDTH, RG_WIDTH)),
            lay((RG_WIDTH, RG_WIDTH)),
            lay((D_MODEL, D_MODEL)),
        ],
        out_specs=pl.BlockSpec((BATCH, 2 * TC, D_MODEL), lambda i: (0, i, 0)),
        out_shape=jax.ShapeDtypeStruct((BATCH, SEQ, D_MODEL), f32),
        scratch_shapes=[
            pltpu.VMEM((ROWS, D_PROJ), f32),
            pltpu.VMEM((ROWS, D_PROJ), f32),
            pltpu.VMEM((BATCH, GLA_DV_TOTAL, GLA_DK_TOTAL), f32),
            pltpu.VMEM((2 * S5_FLAT // LANES, BATCH, LANES), f32),
            pltpu.VMEM((RG_WIDTH // LANES, BATCH, LANES), f32),
            pltpu.VMEM(((RG_CONV - 1) * BATCH, RG_WIDTH), f32),
            pltpu.VMEM((ROWS, 2 * S5_FLAT), f32),
            pltpu.VMEM((ROWS, RG_WIDTH), f32),
            pltpu.VMEM((ROWS, RG_WIDTH), f32),
            pltpu.VMEM((ROWS, D_MODEL), bf16),
        ] + [pltpu.VMEM((S5_WIDTH // LANES, BATCH * PITCH, LANES), f32)] * 5 + [
            pltpu.VMEM((ROWS, D_MODEL), bf16)],
        compiler_params=pltpu.CompilerParams(
            dimension_semantics=("arbitrary",), vmem_limit_bytes=VMEM_LIMIT),
        name="mixer",
    )(x3, x3, attn_g, prm["w_in"], prm["wgk"], prm["vec"], prm["gla_norm"], prm["lam_bar"], prm["bd"], prm["cd"],
      prm["w_glu"], prm["wa"], prm["wx"], prm["w_out"])


def _ffn_kernel(x_ref, g_ref, wup_ref, cw_ref, cb_ref, wdn_ref, fg_ref, o_ref, carry_ref, act_ref, *, last):
    @pl.when(pl.program_id(1) == 0)
    def _():
        carry_ref[...] = jnp.zeros_like(carry_ref)

    x = x_ref[...]
    h = _rms(x, g_ref[...]).astype(bf16)
    row = lax.broadcasted_iota(jnp.int32, (TM, FF_CHUNK), 0)
    for j in range(D_FF // FF_CHUNK):
        cols = slice(j * FF_CHUNK, (j + 1) * FF_CHUNK)
        up = _dot(h, wup_ref[:, cols])
        gv = _dot(h, wup_ref[:, D_FF + j * FF_CHUNK:D_FF + (j + 1) * FF_CHUNK])
        conv = up * cw_ref[FFN_CONV - 1:FFN_CONV, cols] + cb_ref[:, cols]
        for s in range(1, FFN_CONV):
            shifted = pltpu.roll(up, s, 0)
            for t in range(s):
                shifted = jnp.where(row == t, carry_ref[SUBLANES - s + t:SUBLANES - s + t + 1, cols], shifted)
            conv = conv + shifted * cw_ref[FFN_CONV - 1 - s:FFN_CONV - s, cols]
        carry_ref[:, cols] = up[TM - SUBLANES:, :]
        act_ref[:, cols] = (jax.nn.gelu(conv) * gv).astype(bf16)
    out = x + _dot(act_ref[...], wdn_ref[...])
    o_ref[...] = _rms(out, fg_ref[...]) if last else out


def _ffn(x3, mlp_norm, w_up, conv_w, conv_b, w_down, final_g, layer):
    def lay(shape):
        return pl.BlockSpec((None,) + shape, lambda b, i: (layer,) + (0,) * len(shape),
                            pipeline_mode=pl.Buffered(1))

    return pl.pallas_call(
        functools.partial(_ffn_kernel, last=(layer == DEPTH - 1)),
        grid=(BATCH, SEQ // TM),
        in_specs=[
            pl.BlockSpec((None, TM, D_MODEL), lambda b, i: (b, i, 0)),
            lay((1, D_MODEL)),
            lay((D_MODEL, 2 * D_FF)),
            lay((FFN_CONV, D_FF)),
            lay((1, D_FF)),
            lay((D_FF, D_MODEL)),
            pl.BlockSpec((1, D_MODEL), lambda b, i: (0, 0)),
        ],
        out_specs=pl.BlockSpec((None, TM, D_MODEL), lambda b, i: (b, i, 0)),
        out_shape=jax.ShapeDtypeStruct((BATCH, SEQ, D_MODEL), f32),
        scratch_shapes=[pltpu.VMEM((SUBLANES, D_FF), f32),
                        pltpu.VMEM((TM, D_FF), bf16)],
        compiler_params=pltpu.CompilerParams(
            dimension_semantics=("arbitrary", "arbitrary"), vmem_limit_bytes=VMEM_LIMIT),
        name="ffn",
    )(x3, mlp_norm, w_up, conv_w, conv_b, w_down, final_g)


def _block_diag(w):
    nl, nh, n, m = w.shape
    eye = jnp.eye(nh, dtype=w.dtype)
    return jnp.einsum('lhnm,hk->lhnkm', w, eye).reshape(nl, nh * n, nh * m)


def _prepare(w_in, gla_w_gk_up, gla_b_gk, gla_norm, s5_lambda_re, s5_lambda_im, s5_log_dt,
             s5_b_re, s5_b_im, s5_c_re, s5_c_im, s5_d, s5_w_glu, s5_b_glu, s5_norm,
             rg_conv_w, rg_conv_b, rg_w_a, rg_b_a, rg_w_x, rg_b_x, rg_lambda, rg_norm, w_out):
    nl = w_in.shape[0]
    wgk = jnp.concatenate(
        [gla_w_gk_up, jnp.zeros((nl, LANES - GLA_LOW_RANK, GLA_DK_TOTAL), f32)], axis=1)

    lam = lax.complex(s5_lambda_re, s5_lambda_im)
    dt = jnp.exp(s5_log_dt)[..., None]
    lam_bar = jnp.exp(lam * dt)
    b_bar = ((lam_bar - 1.0) / lam)[..., None] * lax.complex(s5_b_re, s5_b_im)
    bt = jnp.swapaxes(b_bar, 2, 3)
    bd = jnp.concatenate([_block_diag(bt.real), _block_diag(bt.imag)], axis=-1).astype(bf16)
    ct = jnp.swapaxes(lax.complex(s5_c_re, s5_c_im), 2, 3)
    cd = jnp.concatenate([_block_diag(ct.real), -_block_diag(ct.imag)], axis=1).astype(bf16)
    lam_flat = jnp.stack([lam_bar.real.reshape(nl, S5_FLAT), lam_bar.imag.reshape(nl, S5_FLAT)], axis=1)

    rows = [gla_b_gk, s5_d, s5_b_glu, s5_norm,
            rg_conv_w[:, 0], rg_conv_w[:, 1], rg_conv_w[:, 2], rg_conv_w[:, 3],
            rg_conv_b, rg_b_a, rg_b_x, rg_lambda, rg_norm]
    vec = jnp.stack(rows + [jnp.zeros_like(s5_d)] * (VEC_ROWS - len(rows)), axis=1)
    return dict(
        w_in=_win_layout(w_in), wgk=wgk, vec=vec, gla_norm=gla_norm[:, None, :], lam_bar=lam_flat,
        bd=bd, cd=cd, w_glu=s5_w_glu.astype(bf16),
        wa=_block_diag(rg_w_a).astype(bf16), wx=_block_diag(rg_w_x).astype(bf16),
        w_out=w_out.astype(bf16))


def kernel(x, attn_norm, w_in, gla_w_gk_up, gla_b_gk, gla_norm, s5_lambda_re, s5_lambda_im, s5_log_dt, s5_b_re, s5_b_im, s5_c_re, s5_c_im, s5_d, s5_w_glu, s5_b_glu, s5_norm, rg_conv_w, rg_conv_b, rg_w_a, rg_b_a, rg_w_x, rg_b_x, rg_lambda, rg_norm, w_out, mlp_norm, w_up, mlp_conv_w, mlp_conv_b, w_down, final_norm):
    prm = _prepare(w_in, gla_w_gk_up, gla_b_gk, gla_norm, s5_lambda_re, s5_lambda_im, s5_log_dt,
                   s5_b_re, s5_b_im, s5_c_re, s5_c_im, s5_d, s5_w_glu, s5_b_glu, s5_norm,
                   rg_conv_w, rg_conv_b, rg_w_a, rg_b_a, rg_w_x, rg_b_x, rg_lambda, rg_norm, w_out)
    attn_g = attn_norm[:, None, :]
    mlp_g = mlp_norm[:, None, :]
    w_up_b = w_up.astype(bf16)
    w_down_b = w_down.astype(bf16)
    conv_b = mlp_conv_b[:, None, :]
    for layer in range(DEPTH):
        x = _mixer(x, attn_g, prm, layer)
        x = _ffn(x, mlp_g, w_up_b, mlp_conv_w, conv_b, w_down_b, final_norm[None, :], layer)
    return x
```
